```python
import jax, jax.numpy as jnp
from jax import lax
import numpy as np

D_MODEL = 2048
BATCH = 2
SEQ = 8192
DEPTH = 2

CTX_LEN = 256
GRID_W = 64
N_MIXERS = 2
MIXER_MLSTM = 0
MIXER_CHUNK_MLP = 1
EPS = 1e-6

ML_HEADS = 4
ML_DQK = 256
ML_DV = D_MODEL // ML_HEADS
ML_CHUNK = 128
ML_CONV = 3
ML_QK_W = ML_HEADS * ML_DQK
ML_V_W = ML_HEADS * ML_DV
ML_IN_W = 2 * ML_QK_W + 2 * ML_V_W + 4 * ML_HEADS

CM_CHUNK = 128
CM_WIDTH = D_MODEL
CM_GROUPS = 8
CM_GDIM = CM_WIDTH // CM_GROUPS

PEER_HEADS = 8
PEER_NKEYS = 128
PEER_EXPERTS = PEER_NKEYS * PEER_NKEYS
PEER_DKEY = 256
PEER_HALF = PEER_DKEY // 2
PEER_TOPK = 16
PEER_BLOCK = 128

kernel_name = 'hybrid_mlstm_chunkmlp_peer_dit'


def rmsnorm(x, g):
    xf = x.astype(jnp.float32)
    y = xf * lax.rsqrt(jnp.mean(xf * xf, axis=-1, keepdims=True) + EPS)
    return (y * g.astype(jnp.float32)).astype(x.dtype)


def adaln(x, g, shift, scale):
    return rmsnorm(x, g) * (1 + scale[:, None, :]) + shift[:, None, :]


def modulation(cond, w_mod, b_mod):
    m = jax.nn.silu(cond) @ w_mod + b_mod
    return jnp.split(m, 6, axis=-1)


def grid_conv(x, w, b):
    bsz, t, ch = x.shape
    rows = t // GRID_W
    xg = x.reshape(bsz, rows, GRID_W, ch)
    y = lax.conv_general_dilated(xg, w.astype(x.dtype)[:, :, None, :], (1, 1), 'SAME',
                                 dimension_numbers=('NHWC', 'HWIO', 'NHWC'),
                                 feature_group_count=ch)
    return y.reshape(bsz, t, ch) + b


def seq_conv(x, w, b):
    t = x.shape[1]
    pad = ML_CONV // 2
    xp = jnp.pad(x, ((0, 0), (pad, pad), (0, 0)))
    taps = w[pad]
    y = xp[:, 0:t] * taps[0]
    for j in range(1, ML_CONV):
        y = y + xp[:, j:j + t] * taps[j]
    return y + b


def zero_state(bsz):
    return (jnp.zeros((bsz, ML_HEADS, ML_DV, ML_DQK), jnp.float32),
            jnp.zeros((bsz, ML_HEADS, ML_DQK), jnp.float32),
            jnp.zeros((bsz, ML_HEADS), jnp.float32))


def mlstm_scan(q, k, v, log_i, log_f, state, emit):
    bsz, nh, t, _ = q.shape
    nc = t // ML_CHUNK

    def chunks(a):
        a = a.reshape((bsz, nh, nc, ML_CHUNK) + a.shape[3:])
        return jnp.moveaxis(a, 2, 0)

    tril = jnp.tril(jnp.ones((ML_CHUNK, ML_CHUNK), dtype=bool))

    def step(carry, inp):
        c_mat, n_vec, m = carry
        qc, kc, vc, ic, fc = inp
        b = jnp.cumsum(fc, axis=-1)
        b_end = b[..., -1]
        w_log = b_end[..., None] - b + ic
        m_new = jnp.maximum(b_end + m, jnp.max(w_log, axis=-1))
        a_state = jnp.exp(b_end + m - m_new)
        w = jnp.exp(w_log - m_new[..., None])
        c_new = a_state[..., None, None] * c_mat + jnp.einsum('bhs,bhse,bhsd->bhed', w, vc, kc)
        n_new = a_state[..., None] * n_vec + jnp.einsum('bhs,bhsd->bhd', w, kc)
        if not emit:
            return (c_new, n_new, m_new), None
        inter = b + m[..., None]
        d_log = jnp.where(tril, b[..., :, None] - b[..., None, :] + ic[..., None, :], -jnp.inf)
        m_t = jnp.maximum(inter, jnp.max(d_log, axis=-1))
        a_inter = jnp.exp(inter - m_t)
        p = jnp.exp(d_log - m_t[..., None]) * jnp.einsum('bhtd,bhsd->bhts', qc, kc)
        num = (a_inter[..., None] * jnp.einsum('bhed,bhtd->bhte', c_mat, qc)
               + jnp.einsum('bhts,bhse->bhte', p, vc))
        den = a_inter * jnp.einsum('bhd,bhtd->bht', n_vec, qc) + jnp.sum(p, axis=-1)
        h = num / jnp.maximum(jnp.abs(den), jnp.exp(-m_t))[..., None]
        return (c_new, n_new, m_new), h

    state, hs = lax.scan(step, state, (chunks(q), chunks(k), chunks(v), chunks(log_i), chunks(log_f)))
    if not emit:
        return None, state
    h = jnp.moveaxis(hs, 0, 2).reshape(bsz, nh, t, ML_DV)
    return h, state


def mlstm_stream(hn, state_f, state_b, on_grid, emit, w_in, conv_w, conv_b, gate_b, head_g, w_out):
    bsz, t, _ = hn.shape
    proj = hn @ w_in
    qk, v, o, gates = jnp.split(proj, [2 * ML_QK_W, 2 * ML_QK_W + ML_V_W, 2 * ML_QK_W + 2 * ML_V_W], axis=-1)
    qk = jax.nn.silu(grid_conv(qk, conv_w, conv_b) if on_grid else seq_conv(qk, conv_w, conv_b))
    q, k = jnp.split(qk, 2, axis=-1)

    def heads(a, d):
        return a.reshape(bsz, t, ML_HEADS, d).transpose(0, 2, 1, 3).astype(jnp.float32)

    q = heads(q, ML_DQK)
    k = heads(k, ML_DQK) * (ML_DQK ** -0.5)
    v = heads(v, ML_DV)
    g = (gates.astype(jnp.float32) + gate_b.astype(jnp.float32)).reshape(bsz, t, 4, ML_HEADS)
    g = g.transpose(2, 0, 3, 1)
    log_i_f, log_f_f = g[0], jax.nn.log_sigmoid(g[1])
    log_i_b, log_f_b = g[2], jax.nn.log_sigmoid(g[3])
    flip = lambda a: jnp.flip(a, axis=2)
    h_f, st_f = mlstm_scan(q, k, v, log_i_f, log_f_f, state_f, emit)
    h_b, st_b = mlstm_scan(flip(q), flip(k), flip(v), flip(log_i_b), flip(log_f_b), state_b, emit)
    if not emit:
        return None, st_f, st_b
    h = h_f + flip(h_b)
    h = h * lax.rsqrt(jnp.mean(h * h, axis=-1, keepdims=True) + EPS) * head_g.astype(jnp.float32)[None, :, None, :]
    h = h.transpose(0, 2, 1, 3).reshape(bsz, t, ML_V_W).astype(hn.dtype) * jax.nn.sigmoid(o)
    return h @ w_out, st_f, st_b


def chunk_mlp(hn, w_in, g_v, w_s, b_s, w_out):
    bsz, t, _ = hn.shape
    nc = t // CM_CHUNK
    z = jax.nn.gelu(hn @ w_in)
    u, v = jnp.split(z, 2, axis=-1)
    v = rmsnorm(v, g_v).reshape(bsz, nc, CM_CHUNK, CM_GROUPS, CM_GDIM)
    sv = jnp.einsum('gts,bcsgd->bctgd', w_s, v) + b_s.T[None, None, :, :, None]
    return (u * sv.reshape(bsz, t, CM_WIDTH)) @ w_out


def peer(hn, w_q, sub_keys, u_emb, v_emb):
    bsz, t, d = hn.shape
    tok = hn.reshape(-1, PEER_BLOCK, d)

    def block(xb):
        q = (xb @ w_q).reshape(PEER_BLOCK, PEER_HEADS, 2, PEER_HALF)
        s = jnp.einsum('thpd,hpnd->thpn', q, sub_keys).astype(jnp.float32)
        s1, i1 = lax.top_k(s[:, :, 0], PEER_TOPK)
        s2, i2 = lax.top_k(s[:, :, 1], PEER_TOPK)
        cand = (s1[..., :, None] + s2[..., None, :]).reshape(PEER_BLOCK, PEER_HEADS, PEER_TOPK * PEER_TOPK)
        cidx = (i1[..., :, None] * PEER_NKEYS + i2[..., None, :]).reshape(PEER_BLOCK, PEER_HEADS, PEER_TOPK * PEER_TOPK)
        top_s, pos = lax.top_k(cand, PEER_TOPK)
        eidx = jnp.take_along_axis(cidx, pos, axis=-1)
        gate = jax.nn.softmax(top_s, axis=-1)
        ue = u_emb[eidx]
        ve = v_emb[eidx]
        act = jax.nn.gelu(jnp.einsum('thkd,td->thk', ue, xb).astype(jnp.float32)) * gate
        return jnp.einsum('thk,thkd->td', act.astype(xb.dtype), ve)

    return lax.map(block, tok).reshape(bsz, t, d)


def setup_inputs(seed: int = 0) -> dict:
    key = jax.random.key(seed)
    ks = iter(jax.random.split(key, 40))
    n_ml = len(range(MIXER_MLSTM, DEPTH, N_MIXERS))
    n_cm = len(range(MIXER_CHUNK_MLP, DEPTH, N_MIXERS))

    def nrm(shape, scale):
        return jax.random.normal(next(ks), shape, jnp.float32) * scale

    def gain(shape):
        return 1.0 + nrm(shape, 0.02)

    gi_f = nrm((n_ml, ML_HEADS), 0.1)
    gf_f = 3.0 + 3.0 * jax.random.uniform(next(ks), (n_ml, ML_HEADS), jnp.float32)
    gi_b = nrm((n_ml, ML_HEADS), 0.1)
    gf_b = 3.0 + 3.0 * jax.random.uniform(next(ks), (n_ml, ML_HEADS), jnp.float32)
    return {
        'x': nrm((BATCH, SEQ, D_MODEL), 1.0),
        'c': nrm((BATCH, D_MODEL), 1.0),
        'ctx': nrm((BATCH, CTX_LEN, D_MODEL), 1.0),
        'c_ctx': nrm((D_MODEL,), 1.0),
        'w_mod': nrm((DEPTH, D_MODEL, 6 * D_MODEL), 0.5 * D_MODEL ** -0.5),
        'b_mod': nrm((DEPTH, 6 * D_MODEL), 0.02),
        'g_mix': gain((DEPTH, D_MODEL)),
        'g_chan': gain((DEPTH, D_MODEL)),
        'ml_w_in': nrm((n_ml, D_MODEL, ML_IN_W), D_MODEL ** -0.5),
        'ml_conv_w': nrm((n_ml, ML_CONV, ML_CONV, 2 * ML_QK_W), 1.0 / ML_CONV),
        'ml_conv_b': nrm((n_ml, 2 * ML_QK_W), 0.02),
        'ml_gate_b': jnp.concatenate([gi_f, gf_f, gi_b, gf_b], axis=-1),
        'ml_head_g': gain((n_ml, ML_HEADS, ML_DV)),
        'ml_w_out': nrm((n_ml, ML_V_W, D_MODEL), ML_V_W ** -0.5),
        'cm_w_in': nrm((n_cm, D_MODEL, 2 * CM_WIDTH), D_MODEL ** -0.5),
        'cm_g_v': gain((n_cm, CM_WIDTH)),
        'cm_w_s': nrm((n_cm, CM_GROUPS, CM_CHUNK, CM_CHUNK), CM_CHUNK ** -0.5),
        'cm_b_s': gain((n_cm, CM_GROUPS, CM_CHUNK)),
        'cm_w_out': nrm((n_cm, CM_WIDTH, D_MODEL), CM_WIDTH ** -0.5),
        'peer_w_q': nrm((DEPTH, D_MODEL, PEER_HEADS * PEER_DKEY), D_MODEL ** -0.5),
        'peer_keys': nrm((DEPTH, PEER_HEADS, 2, PEER_NKEYS, PEER_HALF), PEER_HALF ** -0.5),
        'peer_u': nrm((DEPTH, PEER_EXPERTS, D_MODEL), D_MODEL ** -0.5),
        'peer_v': nrm((DEPTH, PEER_EXPERTS, D_MODEL), 1.0),
        'g_final': gain((D_MODEL,)),
    }


def reference(x, c, ctx, c_ctx, w_mod, b_mod, g_mix, g_chan,
              ml_w_in, ml_conv_w, ml_conv_b, ml_gate_b, ml_head_g, ml_w_out,
              cm_w_in, cm_g_v, cm_w_s, cm_b_s, cm_w_out,
              peer_w_q, peer_keys, peer_u, peer_v, g_final):
    bsz = x.shape[0]
    h_lat = x
    h_ctx = ctx
    for i in range(DEPTH):
        kind = i % N_MIXERS
        j = i // N_MIXERS
        ctx_later = any(l % N_MIXERS == MIXER_MLSTM for l in range(i + 1, DEPTH))
        ctx_now = ctx_later or kind == MIXER_MLSTM
        sh1, sc1, gt1, sh2, sc2, gt2 = modulation(c, w_mod[i], b_mod[i])
        hn = adaln(h_lat, g_mix[i], sh1, sc1)
        if ctx_now:
            csh1, csc1, cgt1, csh2, csc2, cgt2 = modulation(c_ctx[None, :], w_mod[i], b_mod[i])
            hc = adaln(h_ctx, g_mix[i], csh1, csc1)
        if kind == MIXER_MLSTM:
            ml = (ml_w_in[j], ml_conv_w[j], ml_conv_b[j], ml_gate_b[j], ml_head_g[j], ml_w_out[j])
            z0 = zero_state(bsz)
            yc, st_f, st_b = mlstm_stream(hc, z0, z0, False, ctx_later, *ml)
            y, _, _ = mlstm_stream(hn, st_f, st_b, True, True, *ml)
        else:
            cm = (cm_w_in[j], cm_g_v[j], cm_w_s[j], cm_b_s[j], cm_w_out[j])
            y = chunk_mlp(hn, *cm)
            yc = chunk_mlp(hc, *cm) if ctx_later else None
        pe = (peer_w_q[i], peer_keys[i], peer_u[i], peer_v[i])
        h_lat = h_lat + gt1[:, None, :] * y
        h_lat = h_lat + gt2[:, None, :] * peer(adaln(h_lat, g_chan[i], sh2, sc2), *pe)
        if ctx_later:
            h_ctx = h_ctx + cgt1[:, None, :] * yc
            h_ctx = h_ctx + cgt2[:, None, :] * peer(adaln(h_ctx, g_chan[i], csh2, csc2), *pe)
    return rmsnorm(h_lat, g_final)
```

```python
import functools

import jax
import jax.numpy as jnp
from jax import lax
from jax.experimental import pallas as pl
from jax.experimental.pallas import tpu as pltpu

F32 = jnp.float32
BF16 = jnp.bfloat16
EPS = 1e-6

GRID_W = 64
ML_HEADS = 4
ML_DQK = 256
ML_DV = 512
ML_CHUNK = 128
ML_NX = 128
CM_CHUNK = 128
CM_GROUPS = 8
PEER_HEADS = 8
PEER_NKEYS = 128
PEER_HALF = 128
PEER_TOPK = 16
PEER_SEL = PEER_HEADS * PEER_TOPK

VMEM_LIMIT = 48 * 1024 * 1024
HI = lax.Precision.HIGHEST


def _cparams(sem):
    return pltpu.CompilerParams(dimension_semantics=sem, vmem_limit_bytes=VMEM_LIMIT)


def _sigmoid(x):
    return 1.0 / (1.0 + jnp.exp(-x))


def _gelu(x):
    return 0.5 * x * (1.0 + jnp.tanh(0.7978845608028654 * (x + 0.044715 * (x * x * x))))


def _log_sigmoid(x):
    return jnp.minimum(x, 0.0) - jnp.log(1.0 + jnp.exp(-jnp.abs(x)))


def _adaln(x, g, sh, sc):
    ms = jnp.mean(x * x, axis=-1, keepdims=True)
    return (x * lax.rsqrt(ms + EPS) * g) * (1.0 + sc) + sh


def _mod_kernel(c_ref, w_ref, b_ref, o_ref):
    a = c_ref[...]
    a = a * _sigmoid(a)
    o_ref[0] = jnp.dot(a, w_ref[0], preferred_element_type=F32) + b_ref[0]


def modulation(cond, w_mod, b_mod, tn=1024):
    nl, d, n = w_mod.shape
    return pl.pallas_call(
        _mod_kernel,
        grid=(nl, n // tn),
        in_specs=[pl.BlockSpec((8, d), lambda l, j: (0, 0)),
                  pl.BlockSpec((1, d, tn), lambda l, j: (l, 0, j)),
                  pl.BlockSpec((1, 1, tn), lambda l, j: (l, 0, j))],
        out_specs=pl.BlockSpec((1, 8, tn), lambda l, j: (l, 0, j)),
        out_shape=jax.ShapeDtypeStruct((nl, 8, n), F32),
        compiler_params=_cparams(("parallel", "parallel")),
        name="modulation",
    )(cond, w_mod, b_mod.reshape(nl, 1, n))


def _ln_mm_kernel(x_ref, g_ref, sh_ref, sc_ref, w_ref, o_ref, hn_ref, *, act):
    @pl.when(pl.program_id(1) == 0)
    def _():
        hn_ref[...] = _adaln(x_ref[...], g_ref[...], sh_ref[0], sc_ref[0]).astype(BF16)

    r = jnp.dot(hn_ref[...], w_ref[...], preferred_element_type=F32)
    if act == "gelu":
        r = _gelu(r)
    o_ref[...] = r.astype(o_ref.dtype)


def ln_matmul(x, g, sh, sc, w, rows_per_mod, act=None, tm=512, tn=1024):
    m, d = x.shape
    n = w.shape[1]
    tm = min(tm, m)
    nb = sh.shape[0]
    mod_map = lambda i, j: ((i * tm) // rows_per_mod, 0, 0)
    return pl.pallas_call(
        functools.partial(_ln_mm_kernel, act=act),
        grid=(m // tm, n // tn),
        in_specs=[pl.BlockSpec((tm, d), lambda i, j: (i, 0)),
                  pl.BlockSpec((1, d), lambda i, j: (0, 0)),
                  pl.BlockSpec((1, 1, d), mod_map),
                  pl.BlockSpec((1, 1, d), mod_map),
                  pl.BlockSpec((d, tn), lambda i, j: (0, j))],
        out_specs=pl.BlockSpec((tm, tn), lambda i, j: (i, j)),
        out_shape=jax.ShapeDtypeStruct((m, n), BF16),
        scratch_shapes=[pltpu.VMEM((tm, d), BF16)],
        compiler_params=_cparams(("parallel", "arbitrary")),
        name="ln_matmul",
    )(x, g.reshape(1, d), sh.reshape(nb, 1, d), sc.reshape(nb, 1, d), w)


def _gates_kernel(x_ref, g_ref, sh_ref, sc_ref, w_ref, wt_ref, b_ref, bt_ref, oc_ref, or_ref):
    hn = _adaln(x_ref[...], g_ref[...], sh_ref[0], sc_ref[0])
    oc_ref[...] = jnp.dot(hn, w_ref[...], preferred_element_type=F32) + b_ref[...]
    or_ref[...] = lax.dot_general(wt_ref[...], hn, (((1,), (1,)), ((), ())),
                                  preferred_element_type=F32) + bt_ref[...]


def gate_preacts(x, g, sh, sc, w_g, b_g, rows_per_mod, tm=512):
    m, d = x.shape
    ng = w_g.shape[1]
    tm = min(tm, m)
    nb = sh.shape[0]
    mod_map = lambda i: ((i * tm) // rows_per_mod, 0, 0)
    return pl.pallas_call(
        _gates_kernel,
        grid=(m // tm,),
        in_specs=[pl.BlockSpec((tm, d), lambda i: (i, 0)),
                  pl.BlockSpec((1, d), lambda i: (0, 0)),
                  pl.BlockSpec((1, 1, d), mod_map),
                  pl.BlockSpec((1, 1, d), mod_map),
                  pl.BlockSpec((d, ng), lambda i: (0, 0)),
                  pl.BlockSpec((ng, d), lambda i: (0, 0)),
                  pl.BlockSpec((1, ng), lambda i: (0, 0)),
                  pl.BlockSpec((ng, 1), lambda i: (0, 0))],
        out_specs=[pl.BlockSpec((tm, ng), lambda i: (i, 0)),
                   pl.BlockSpec((ng, tm), lambda i: (0, i))],
        out_shape=[jax.ShapeDtypeStruct((m, ng), F32),
                   jax.ShapeDtypeStruct((ng, m), F32)],
        compiler_params=_cparams(("parallel",)),
        name="gate_preacts",
    )(x, g.reshape(1, d), sh.reshape(nb, 1, d), sc.reshape(nb, 1, d),
      w_g, w_g.T, b_g.reshape(1, ng), b_g.reshape(ng, 1))


def _conv_kernel(*refs, on_grid, tt, nt):
    if on_grid:
        xp_ref, xc_ref, xn_ref, w_ref, b_ref, s_ref, o_ref = refs
    else:
        xc_ref, w_ref, b_ref, s_ref, o_ref = refs
    i = pl.program_id(1)
    ct = xc_ref.shape[-1]
    cur = xc_ref[0].astype(F32)
    tok = lax.broadcasted_iota(jnp.int32, (tt, ct), 0)
    if on_grid:
        first = (tok % GRID_W) == 0
        last = (tok % GRID_W) == GRID_W - 1
    else:
        first = tok == 0
        last = tok == tt - 1

    def row_taps(base, dr):
        left = jnp.where(first, 0.0, pltpu.roll(base, 1, axis=0))
        right = jnp.where(last, 0.0, pltpu.roll(base, tt - 1, axis=0))
        return left * w_ref[dr, 0:1, :] + base * w_ref[dr, 1:2, :] + right * w_ref[dr, 2:3, :]

    y = row_taps(cur, 1)
    if on_grid:
        prev = jnp.where(i > 0, xp_ref[0].astype(F32), 0.0)
        nxt = jnp.where(i < nt - 1, xn_ref[0].astype(F32), 0.0)
        up = jnp.concatenate([prev, cur[:tt - GRID_W]], axis=0)
        down = jnp.concatenate([cur[GRID_W:], nxt], axis=0)
        y = y + row_taps(up, 0) + row_taps(down, 2)
    y = y + b_ref[...]
    y = y * _sigmoid(y)
    o_ref[0] = (y * s_ref[...]).astype(o_ref.dtype)


def conv_silu(proj, conv_w, conv_b, out_scale, n_ch, on_grid, tt=512, ct=512):
    bsz, t, _ = proj.shape
    if not on_grid:
        tt = t
    nt = t // tt
    rpt = tt // GRID_W
    kern = functools.partial(_conv_kernel, on_grid=on_grid, tt=tt, nt=nt)
    cur_spec = pl.BlockSpec((1, tt, ct), lambda b, i, j: (b, i, j))
    par_specs = [pl.BlockSpec((3, 3, ct), lambda b, i, j: (0, 0, j)),
                 pl.BlockSpec((1, ct), lambda b, i, j: (0, j)),
                 pl.BlockSpec((1, ct), lambda b, i, j: (0, j))]
    if on_grid:
        nrow = t // GRID_W
        in_specs = [pl.BlockSpec((1, GRID_W, ct), lambda b, i, j: (b, jnp.maximum(i * rpt - 1, 0), j)),
                    cur_spec,
                    pl.BlockSpec((1, GRID_W, ct), lambda b, i, j: (b, jnp.minimum((i + 1) * rpt, nrow - 1), j))]
        args = (proj, proj, proj)
    else:
        in_specs = [cur_spec]
        args = (proj,)
    return pl.pallas_call(
        kern,
        grid=(bsz, nt, n_ch // ct),
        in_specs=in_specs + par_specs,
        out_specs=pl.BlockSpec((1, tt, ct), lambda b, i, j: (b, i, j)),
        out_shape=jax.ShapeDtypeStruct((bsz, t, n_ch), BF16),
        compiler_params=_cparams(("parallel", "parallel", "parallel")),
        name="conv_silu_grid" if on_grid else "conv_silu_seq",
    )(*args, conv_w, conv_b.reshape(1, n_ch), out_scale.reshape(1, n_ch))


def _mlstm_kernel(*refs, rev, emit, nc):
    if emit:
        (q_ref, k_ref, v_ref, gc_ref, gr_ref, s0_ref, m0_ref,
         h_ref, so_ref, mo_ref, s_scr, m_scr) = refs
    else:
        (k_ref, v_ref, gc_ref, gr_ref, s0_ref, m0_ref,
         so_ref, mo_ref, s_scr, m_scr) = refs
    c = pl.program_id(2)
    ln = ML_CHUNK
    gi = 2 if rev else 0

    @pl.when(c == 0)
    def _():
        s_scr[...] = s0_ref[0, 0]
        m_scr[...] = m0_ref[0, 0]

    gc = gc_ref[0, 0]
    gr = gr_ref[0, 0]
    i_col = gc[:, gi:gi + 1]
    f_col = _log_sigmoid(gc[:, gi + 1:gi + 2])
    i_row = gr[gi:gi + 1, :]
    f_row = _log_sigmoid(gr[gi + 1:gi + 2, :])
    row = lax.broadcasted_iota(jnp.int32, (ln, ln), 0)
    col = lax.broadcasted_iota(jnp.int32, (ln, ln), 1)
    seen = (col >= row) if rev else (col <= row)
    upto = (row >= col) if rev else (row <= col)
    b_col = jnp.sum(jnp.where(seen, f_row, 0.0), axis=1, keepdims=True)
    b_row = jnp.sum(jnp.where(upto, f_col, 0.0), axis=0, keepdims=True)
    b_end = jnp.sum(f_row, axis=1, keepdims=True)
    m_old = m_scr[0:1, 0:1]
    w_log = b_end - b_row + i_row
    m_new = jnp.maximum(b_end + m_old, jnp.max(w_log, axis=1, keepdims=True))
    a_state = jnp.exp(b_end + m_old - m_new)
    w_col = jnp.exp(b_end - b_col + i_col - m_new)

    kc = k_ref[0]
    vc = v_ref[0]
    lane = lax.broadcasted_iota(jnp.int32, (ln, ML_NX), 1)
    ones_col = jnp.where(lane == 0, 1.0, 0.0)
    vx = jnp.concatenate([vc.astype(F32), ones_col], axis=1)
    s_old = s_scr[...]

    if emit:
        qc = q_ref[0]
        inter = b_col + m_old
        d_log = jnp.where(seen, b_col - b_row + i_row, -jnp.inf)
        m_t = jnp.maximum(inter, jnp.max(d_log, axis=1, keepdims=True))
        a_inter = jnp.exp(inter - m_t)
        qk = lax.dot_general(qc, kc, (((1,), (1,)), ((), ())), preferred_element_type=F32)
        p = jnp.exp(d_log - m_t) * qk
        num = a_inter * jnp.dot(qc, s_old.astype(BF16), preferred_element_type=F32) \
            + jnp.dot(p.astype(BF16), vx.astype(BF16), preferred_element_type=F32)
        den = num[:, ML_DV:ML_DV + 1]
        h_ref[0] = num[:, :ML_DV] / jnp.maximum(jnp.abs(den), jnp.exp(-m_t))

    vw = (w_col * vx).astype(BF16)
    kt = kc.astype(F32).T.astype(BF16)
    s_scr[...] = a_state * s_old + jnp.dot(kt, vw, preferred_element_type=F32)
    m_scr[...] = jnp.broadcast_to(m_new, m_scr.shape)

    @pl.when(c == nc - 1)
    def _():
        so_ref[0, 0] = s_scr[...]
        mo_ref[0, 0] = m_scr[...]


def mlstm_scan(qk_act, proj, gcol, grow, s0, m0, rev, emit):
    bsz, t, _ = qk_act.shape
    nc = t // ML_CHUNK
    sw = ML_DV + ML_NX
    cm = (lambda c: nc - 1 - c) if rev else (lambda c: c)
    kq_blocks = ML_HEADS
    v_blocks = (2 * ML_HEADS * ML_DQK) // ML_DV
    in_specs = [
        pl.BlockSpec((1, ML_CHUNK, ML_DQK), lambda b, h, c: (b, cm(c), kq_blocks + h)),
        pl.BlockSpec((1, ML_CHUNK, ML_DV), lambda b, h, c: (b, cm(c), v_blocks + h)),
        pl.BlockSpec((1, 1, ML_CHUNK, 4), lambda b, h, c: (b, h, cm(c), 0)),
        pl.BlockSpec((1, 1, 4, ML_CHUNK), lambda b, h, c: (b, h, 0, cm(c))),
        pl.BlockSpec((1, 1, ML_DQK, sw), lambda b, h, c: (b, h, 0, 0)),
        pl.BlockSpec((1, 1, 8, 128), lambda b, h, c: (b, h, 0, 0)),
    ]
    args = [qk_act, proj, gcol, grow, s0, m0]
    out_specs = [pl.BlockSpec((1, 1, ML_DQK, sw), lambda b, h, c: (b, h, 0, 0)),
                 pl.BlockSpec((1, 1, 8, 128), lambda b, h, c: (b, h, 0, 0))]
    out_shape = [jax.ShapeDtypeStruct((bsz, ML_HEADS, ML_DQK, sw), F32),
                 jax.ShapeDtypeStruct((bsz, ML_HEADS, 8, 128), F32)]
    if emit:
        in_specs = [pl.BlockSpec((1, ML_CHUNK, ML_DQK), lambda b, h, c: (b, cm(c), h))] + in_specs
        args = [qk_act] + args
        out_specs = [pl.BlockSpec((1, ML_CHUNK, ML_DV), lambda b, h, c: (b, cm(c), h))] + out_specs
        out_shape = [jax.ShapeDtypeStruct((bsz, t, ML_HEADS * ML_DV), F32)] + out_shape
    outs = pl.pallas_call(
        functools.partial(_mlstm_kernel, rev=rev, emit=emit, nc=nc),
        grid=(bsz, ML_HEADS, nc),
        in_specs=in_specs,
        out_specs=out_specs,
        out_shape=out_shape,
        scratch_shapes=[pltpu.VMEM((ML_DQK, sw), F32), pltpu.VMEM((8, 128), F32)],
        compiler_params=_cparams(("parallel", "parallel", "arbitrary")),
        name="mlstm_" + ("bwd" if rev else "fwd") + ("_emit" if emit else "_state"),
    )(*args)
    if emit:
        return outs[0], outs[1], outs[2]
    return None, outs[0], outs[1]


def _ml_out_kernel(hf_ref, hb_ref, o_ref, hg_ref, w_ref, res_ref, gt_ref, out_ref, y_ref):
    @pl.when(pl.program_id(1) == 0)
    def _():
        h = hf_ref[...] + hb_ref[...]
        parts = []
        for hd in range(ML_HEADS):
            seg = h[:, hd * ML_DV:(hd + 1) * ML_DV]
            ms = jnp.mean(seg * seg, axis=-1, keepdims=True)
            parts.append(seg * lax.rsqrt(ms + EPS))
        hn = jnp.concatenate(parts, axis=1) * hg_ref[...]
        y_ref[...] = (hn * _sigmoid(o_ref[...].astype(F32))).astype(BF16)

    y = jnp.dot(y_ref[...], w_ref[...], preferred_element_type=F32)
    out_ref[...] = res_ref[...] + gt_ref[0] * y


def mlstm_out(h_f, h_b, proj, head_g, w_out, res, gt, rows_per_mod, tm=512, tn=1024):
    m, d = res.shape
    hw = h_f.shape[1]
    o_block = proj.shape[1] // hw - 1
    nb = gt.shape[0]
    mod_map = lambda i, j: ((i * tm) // rows_per_mod, 0, j)
    return pl.pallas_call(
        _ml_out_kernel,
        grid=(m // tm, d // tn),
        in_specs=[pl.BlockSpec((tm, hw), lambda i, j: (i, 0)),
                  pl.BlockSpec((tm, hw), lambda i, j: (i, 0)),
                  pl.BlockSpec((tm, hw), lambda i, j: (i, o_block)),
                  pl.BlockSpec((1, hw), lambda i, j: (0, 0)),
                  pl.BlockSpec((hw, tn), lambda i, j: (0, j)),
                  pl.BlockSpec((tm, tn), lambda i, j: (i, j)),
                  pl.BlockSpec((1, 1, tn), mod_map)],
        out_specs=pl.BlockSpec((tm, tn), lambda i, j: (i, j)),
        out_shape=jax.ShapeDtypeStruct((m, d), F32),
        scratch_shapes=[pltpu.VMEM((tm, hw), BF16)],
        compiler_params=_cparams(("parallel", "arbitrary")),
        name="mlstm_out",
    )(h_f, h_b, proj, head_g.reshape(1, hw), w_out, res, gt.reshape(nb, 1, d))


def _cm_out_kernel(z_ref, gv_ref, ws_ref, bs_ref, w_ref, res_ref, gt_ref, out_ref, y_ref, *, tm):
    width = gv_ref.shape[-1]
    gdim = width // CM_GROUPS

    @pl.when(pl.program_id(1) == 0)
    def _():
        for ch in range(tm // CM_CHUNK):
            rows = slice(ch * CM_CHUNK, (ch + 1) * CM_CHUNK)
            v = z_ref[rows, width:].astype(F32)
            v = v * lax.rsqrt(jnp.mean(v * v, axis=-1, keepdims=True) + EPS) * gv_ref[...]
            for g in range(CM_GROUPS):
                cols = slice(g * gdim, (g + 1) * gdim)
                sv = jnp.dot(ws_ref[g], v[:, cols].astype(BF16), preferred_element_type=F32) \
                    + bs_ref[:, g:g + 1]
                y_ref[rows, cols] = (z_ref[rows, cols].astype(F32) * sv).astype(BF16)

    y = jnp.dot(y_ref[...], w_ref[...], preferred_element_type=F32)
    out_ref[...] = res_ref[...] + gt_ref[0] * y


def chunk_mlp_out(z, g_v, w_s, b_s, w_out, res, gt, rows_per_mod, tm=512, tn=1024):
    m, d = res.shape
    width = g_v.shape[0]
    nb = gt.shape[0]
    mod_map = lambda i, j: ((i * tm) // rows_per_mod, 0, j)
    return pl.pallas_call(
        functools.partial(_cm_out_kernel, tm=tm),
        grid=(m // tm, d // tn),
        in_specs=[pl.BlockSpec((tm, 2 * width), lambda i, j: (i, 0)),
                  pl.BlockSpec((1, width), lambda i, j: (0, 0)),
                  pl.BlockSpec((CM_GROUPS, CM_CHUNK, CM_CHUNK), lambda i, j: (0, 0, 0)),
                  pl.BlockSpec((CM_CHUNK, CM_GROUPS), lambda i, j: (0, 0)),
                  pl.BlockSpec((width, tn), lambda i, j: (0, j)),
                  pl.BlockSpec((tm, tn), lambda i, j: (i, j)),
                  pl.BlockSpec((1, 1, tn), mod_map)],
        out_specs=pl.BlockSpec((tm, tn), lambda i, j: (i, j)),
        out_shape=jax.ShapeDtypeStruct((m, d), F32),
        scratch_shapes=[pltpu.VMEM((tm, width), BF16)],
        compiler_params=_cparams(("parallel", "arbitrary")),
        name="chunk_mlp_out",
    )(z, g_v.reshape(1, width), w_s, b_s.T, w_out, res, gt.reshape(nb, 1, d))


def _extract_top(s, payload, n_out):
    nrows = s.shape[0]
    rid = lax.broadcasted_iota(jnp.int32, s.shape, 0)
    vals, pays = [], []
    for _ in range(n_out):
        mx = jnp.max(s, axis=0, keepdims=True)
        pos = jnp.min(jnp.where(s == mx, rid, nrows), axis=0, keepdims=True)
        hit = rid == pos
        vals.append(mx)
        if payload is None:
            pays.append(pos)
        else:
            pays.append(jnp.sum(jnp.where(hit, payload, 0), axis=0, keepdims=True))
        s = jnp.where(hit, -jnp.inf, s)
    return jnp.concatenate(vals, axis=0), jnp.concatenate(pays, axis=0)


def _peer_topk_kernel(h_ref, g_ref, sh_ref, sc_ref, wq_ref, keys_ref, hn_ref, idx_ref, gate_ref):
    hn = _adaln(h_ref[...], g_ref[...], sh_ref[0], sc_ref[0])
    hn_ref[...] = hn
    qt = lax.dot_general(wq_ref[...], hn.astype(BF16), (((1,), (1,)), ((), ())),
                         preferred_element_type=F32)
    idx_rows, gate_rows = [], []
    for hd in range(PEER_HEADS):
        top = []
        for p in range(2):
            hp = hd * 2 + p
            s = jnp.dot(keys_ref[hp], qt[hp * PEER_HALF:(hp + 1) * PEER_HALF, :],
                        preferred_element_type=F32, precision=HI)
            top.append(_extract_top(s, None, PEER_TOPK))
        (s1, i1), (s2, i2) = top
        cand = jnp.concatenate([s1[a:a + 1, :] + s2 for a in range(PEER_TOPK)], axis=0)
        cidx = jnp.concatenate([i1[a:a + 1, :] * PEER_NKEYS + i2 for a in range(PEER_TOPK)], axis=0)
        top_s, eidx = _extract_top(cand, cidx, PEER_TOPK)
        e = jnp.exp(top_s - top_s[0:1, :])
        gate_rows.append(e / jnp.sum(e, axis=0, keepdims=True))
        idx_rows.append(eidx)
    idx_ref[...] = jnp.concatenate(idx_rows, axis=0).T
    gate_ref[...] = jnp.concatenate(gate_rows, axis=0).T


def peer_topk(h, g, sh, sc, wq_t, keys, rows_per_mod, tm=256):
    m, d = h.shape
    tm = min(tm, m)
    nb = sh.shape[0]
    mod_map = lambda i: ((i * tm) // rows_per_mod, 0, 0)
    nq = wq_t.shape[0]
    return pl.pallas_call(
        _peer_topk_kernel,
        grid=(m // tm,),
        in_specs=[pl.BlockSpec((tm, d), lambda i: (i, 0)),
                  pl.BlockSpec((1, d), lambda i: (0, 0)),
                  pl.BlockSpec((1, 1, d), mod_map),
                  pl.BlockSpec((1, 1, d), mod_map),
                  pl.BlockSpec((nq, d), lambda i: (0, 0)),
                  pl.BlockSpec(keys.shape, lambda i: (0, 0, 0))],
        out_specs=[pl.BlockSpec((tm, d), lambda i: (i, 0)),
                   pl.BlockSpec((tm, PEER_SEL), lambda i: (i, 0)),
                   pl.BlockSpec((tm, PEER_SEL), lambda i: (i, 0))],
        out_shape=[jax.ShapeDtypeStruct((m, d), F32),
                   jax.ShapeDtypeStruct((m, PEER_SEL), jnp.int32),
                   jax.ShapeDtypeStruct((m, PEER_SEL), F32)],
        compiler_params=_cparams(("parallel",)),
        name="peer_topk",
    )(h, g.reshape(1, d), sh.reshape(nb, 1, d), sc.reshape(nb, 1, d), wq_t, keys)


PEER_SLOTS = 3
PEER_AHEAD = PEER_SLOTS - 1


def _sublane_sums(vs):
    sub = lax.broadcasted_iota(jnp.int32, (8, 128), 0)
    lo4 = sub < 4
    w = []
    for j in range(4):
        a = jnp.where(lo4, vs[j], vs[j + 4])
        b = jnp.where(lo4, vs[j + 4], vs[j])
        w.append(a + pltpu.roll(b, 4, axis=0))
    lo2 = (sub % 4) < 2
    x = []
    for j in range(2):
        t1 = w[j] + pltpu.roll(w[j], 6, axis=0)
        t2 = w[j + 2] + pltpu.roll(w[j + 2], 2, axis=0)
        x.append(jnp.where(lo2, t1, t2))
    even = (sub % 2) == 0
    t1 = x[0] + pltpu.roll(x[0], 7, axis=0)
    t2 = x[1] + pltpu.roll(x[1], 1, axis=0)
    return jnp.where(even, t1, t2)


def _peer_gather_kernel(idx_hbm, gate_ref, x_ref, res_ref, gt_ref, tab_hbm, out_ref,
                        idx_smem, buf, sems, idx_sem, act_ref, *, tb):
    nrow = x_ref.shape[1]
    nvr = nrow // 8
    blk = pl.program_id(0)

    idx_cp = pltpu.make_async_copy(idx_hbm.at[pl.ds(blk * (tb * PEER_SEL), tb * PEER_SEL)], idx_smem, idx_sem)
    idx_cp.start()
    idx_cp.wait()

    def issue(t, slot):
        base = t * PEER_SEL
        for k in range(PEER_SEL):
            pltpu.make_async_copy(tab_hbm.at[idx_smem[base + k]], buf.at[slot, k], sems.at[slot]).start()

    def wait(slot):
        pltpu.make_async_copy(tab_hbm.at[pl.ds(0, PEER_SEL)], buf.at[slot], sems.at[slot]).wait()

    for t0 in range(PEER_AHEAD):
        issue(t0, t0)

    ones = jnp.ones((128, 128), F32)
    eye = (lax.broadcasted_iota(jnp.int32, (128, 128), 0)
           == lax.broadcasted_iota(jnp.int32, (128, 128), 1)).astype(F32)

    def token(t, carry):
        slot = t % PEER_SLOTS

        @pl.when(t + PEER_AHEAD < tb)
        def _():
            issue(t + PEER_AHEAD, (t + PEER_AHEAD) % PEER_SLOTS)

        wait(slot)
        xs = [x_ref[t, r * 8:(r + 1) * 8, :] for r in range(nvr)]
        zrows = []
        for g8 in range(PEER_SEL // 8):
            ps = []
            for k in range(g8 * 8, g8 * 8 + 8):
                p = buf[slot, k, 0:8, :] * xs[0]
                for r in range(1, nvr):
                    p = p + buf[slot, k, r * 8:(r + 1) * 8, :] * xs[r]
                ps.append(p)
            zrows.append(_sublane_sums(ps))
        zred = jnp.concatenate(zrows, axis=0)
        score = jnp.dot(zred, ones, preferred_element_type=F32, precision=HI)
        gate = jnp.dot(eye * gate_ref[t], ones, preferred_element_type=F32, precision=HI)
        act_ref[...] = _gelu(score) * gate
        nacc = 4
        accs = [[None] * nvr for _ in range(nacc)]
        for k in range(PEER_SEL):
            a = jnp.broadcast_to(act_ref[k:k + 1, :], (8, 128))
            for r in range(nvr):
                term = a * buf[slot, k, nrow + r * 8:nrow + (r + 1) * 8, :]
                prev = accs[k % nacc][r]
                accs[k % nacc][r] = term if prev is None else prev + term
        for r in range(nvr):
            tot = accs[0][r]
            for j in range(1, nacc):
                tot = tot + accs[j][r]
            rows = slice(r * 8, (r + 1) * 8)
            out_ref[t, rows, :] = res_ref[t, rows, :] + gt_ref[0, rows, :] * tot
        return carry

    lax.fori_loop(0, tb, token, 0)


def peer_gather(idx, gate, x, res, gt, table, rows_per_mod, tb=256):
    m, d = x.shape
    tb = min(tb, m)
    nb = gt.shape[0]
    nrow = d // 128
    mod_map = lambda i: ((i * tb) // rows_per_mod, 0, 0)
    tok_spec = pl.BlockSpec((tb, nrow, 128), lambda i: (i, 0, 0))
    out = pl.pallas_call(
        functools.partial(_peer_gather_kernel, tb=tb),
        grid=(m // tb,),
        in_specs=[pl.BlockSpec(memory_space=pl.ANY),
                  pl.BlockSpec((tb, 1, PEER_SEL), lambda i: (i, 0, 0)),
                  tok_spec,
                  tok_spec,
                  pl.BlockSpec((1, nrow, 128), mod_map),
                  pl.BlockSpec(memory_space=pl.ANY)],
        out_specs=tok_spec,
        out_shape=jax.ShapeDtypeStruct((m, nrow, 128), F32),
        scratch_shapes=[pltpu.SMEM((tb * PEER_SEL,), jnp.int32),
                        pltpu.VMEM((PEER_SLOTS, PEER_SEL, 2 * nrow, 128), F32),
                        pltpu.SemaphoreType.DMA((PEER_SLOTS,)),
                        pltpu.SemaphoreType.DMA(()),
                        pltpu.VMEM((PEER_SEL, 128), F32)],
        compiler_params=_cparams(("arbitrary",)),
        name="peer_gather",
    )(idx.reshape(-1), gate.reshape(m, 1, PEER_SEL), x.reshape(m, nrow, 128), res.reshape(m, nrow, 128),
      gt.reshape(nb, nrow, 128), table)
    return out.reshape(m, d)


def _rmsnorm_kernel(x_ref, g_ref, o_ref):
    x = x_ref[...]
    o_ref[...] = x * lax.rsqrt(jnp.mean(x * x, axis=-1, keepdims=True) + EPS) * g_ref[...]


def rmsnorm(x, g, tm=512):
    m, d = x.shape
    return pl.pallas_call(
        _rmsnorm_kernel,
        grid=(m // tm,),
        in_specs=[pl.BlockSpec((tm, d), lambda i: (i, 0)), pl.BlockSpec((1, d), lambda i: (0, 0))],
        out_specs=pl.BlockSpec((tm, d), lambda i: (i, 0)),
        out_shape=jax.ShapeDtypeStruct((m, d), F32),
        compiler_params=_cparams(("parallel",)),
        name="final_rmsnorm",
    )(x, g.reshape(1, d))


def _split_gates(gcol, grow, bsz, t):
    gc = gcol.reshape(bsz, t, 4, ML_HEADS).transpose(0, 3, 1, 2)
    gr = grow.reshape(4, ML_HEADS, bsz, t).transpose(2, 1, 0, 3)
    return gc, gr


def _mlstm_layer(h_lat, h_ctx, mods, cmods, g_mix, w_in, conv_w, conv_b, gate_b, head_g, w_out):
    bsz, t, d = h_lat.shape
    tc = h_ctx.shape[1]
    sh1, sc1, gt1 = mods
    csh1, csc1 = cmods
    qk_w = 2 * ML_HEADS * ML_DQK
    n_main = qk_w + 2 * ML_HEADS * ML_DV
    w_main = w_in[:, :n_main].astype(BF16)
    w_g = w_in[:, n_main:]
    k_scale = jnp.concatenate([jnp.ones((qk_w // 2,), F32), jnp.full((qk_w // 2,), ML_DQK ** -0.5, F32)])
    sw = ML_DV + ML_NX

    xc = h_ctx.reshape(bsz * tc, d)
    projc = ln_matmul(xc, g_mix, csh1, csc1, w_main, bsz * tc).reshape(bsz, tc, n_main)
    gcc, grc = gate_preacts(xc, g_mix, csh1, csc1, w_g, gate_b, bsz * tc)
    gcc, grc = _split_gates(gcc, grc, bsz, tc)
    qkc = conv_silu(projc, conv_w, conv_b, k_scale, qk_w, on_grid=False)
    s0 = jnp.zeros((bsz, ML_HEADS, ML_DQK, sw), F32)
    m0 = jnp.zeros((bsz, ML_HEADS, 8, 128), F32)
    _, sf, mf = mlstm_scan(qkc, projc, gcc, grc, s0, m0, rev=False, emit=False)
    _, sb, mb = mlstm_scan(qkc, projc, gcc, grc, s0, m0, rev=True, emit=False)

    x2 = h_lat.reshape(bsz * t, d)
    proj = ln_matmul(x2, g_mix, sh1, sc1, w_main, t).reshape(bsz, t, n_main)
    gcl, grl = gate_preacts(x2, g_mix, sh1, sc1, w_g, gate_b, t)
    gcl, grl = _split_gates(gcl, grl, bsz, t)
    qk = conv_silu(proj, conv_w, conv_b, k_scale, qk_w, on_grid=True)
    h_f, _, _ = mlstm_scan(qk, proj, gcl, grl, sf, mf, rev=False, emit=True)
    h_b, _, _ = mlstm_scan(qk, proj, gcl, grl, sb, mb, rev=True, emit=True)
    hw = ML_HEADS * ML_DV
    out = mlstm_out(h_f.reshape(bsz * t, hw), h_b.reshape(bsz * t, hw), proj.reshape(bsz * t, n_main),
                    head_g.reshape(-1), w_out.astype(BF16), x2, gt1, t)
    return out


def _chunk_mlp_layer(x2, t, mods, g_mix, w_in, g_v, w_s, b_s, w_out):
    sh1, sc1, gt1 = mods
    z = ln_matmul(x2, g_mix, sh1, sc1, w_in.astype(BF16), t, act="gelu")
    return chunk_mlp_out(z, g_v, w_s.astype(BF16), b_s, w_out.astype(BF16), x2, gt1, t)


def _peer_layer(x2, t, mods, g_chan, w_q, keys, u_emb, v_emb):
    sh2, sc2, gt2 = mods
    d = x2.shape[1]
    hn, idx, gate = peer_topk(x2, g_chan, sh2, sc2, w_q.T.astype(BF16),
                              keys.reshape(PEER_HEADS * 2, PEER_NKEYS, PEER_HALF), t)
    ne = u_emb.shape[0]
    table = jnp.concatenate([u_emb.reshape(ne, d // 128, 128), v_emb.reshape(ne, d // 128, 128)], axis=1)
    return peer_gather(idx, gate, hn, x2, gt2, table, t)


def kernel(x, c, ctx, c_ctx, w_mod, b_mod, g_mix, g_chan, ml_w_in, ml_conv_w, ml_conv_b, ml_gate_b, ml_head_g, ml_w_out, cm_w_in, cm_g_v, cm_w_s, cm_b_s, cm_w_out, peer_w_q, peer_keys, peer_u, peer_v, g_final):
    bsz, t, d = x.shape
    depth = w_mod.shape[0]
    cond = jnp.zeros((8, d), F32).at[:bsz].set(c).at[bsz].set(c_ctx)
    mods = modulation(cond, w_mod, b_mod)
    h2 = x.reshape(bsz * t, d)
    for i in range(depth):
        j = i // 2
        mi = mods[i].reshape(8, 6, d)
        lat = [mi[:bsz, s] for s in range(6)]
        cx = [mi[bsz:bsz + 1, s] for s in range(6)]
        if i % 2 == 0:
            h2 = _mlstm_layer(h2.reshape(bsz, t, d), ctx, lat[:3], cx[:2], g_mix[i], ml_w_in[j], ml_conv_w[j],
                              ml_conv_b[j], ml_gate_b[j], ml_head_g[j], ml_w_out[j])
        else:
            h2 = _chunk_mlp_layer(h2, t, lat[:3], g_mix[i], cm_w_in[j], cm_g_v[j], cm_w_s[j], cm_b_s[j],
                                  cm_w_out[j])
        h2 = _peer_layer(h2, t, lat[3:], g_chan[i], peer_w_q[i], peer_keys[i], peer_u[i], peer_v[i])
    return rmsnorm(h2, g_final).reshape(bsz, t, d)
```

```python
import functools

import jax
import jax.numpy as jnp
from jax import lax
from jax.experimental import pallas as pl
from jax.experimental.pallas import tpu as pltpu

F32 = jnp.float32
BF16 = jnp.bfloat16
EPS = 1e-6

GRID_W = 64
ML_HEADS = 4
ML_DQK = 256
ML_DV = 512
ML_CHUNK = 128
ML_NX = 128
CM_CHUNK = 128
CM_GROUPS = 8
PEER_HEADS = 8
PEER_NKEYS = 128
PEER_HALF = 128
PEER_TOPK = 16
PEER_SEL = PEER_HEADS * PEER_TOPK

VMEM_LIMIT = 48 * 1024 * 1024
HI = lax.Precision.HIGHEST


def _cparams(sem):
    return pltpu.CompilerParams(dimension_semantics=sem, vmem_limit_bytes=VMEM_LIMIT)


def _sigmoid(x):
    return 1.0 / (1.0 + jnp.exp(-x))


def _gelu(x):
    return 0.5 * x * (1.0 + jnp.tanh(0.7978845608028654 * (x + 0.044715 * (x * x * x))))


def _log_sigmoid(x):
    return jnp.minimum(x, 0.0) - jnp.log(1.0 + jnp.exp(-jnp.abs(x)))


def _adaln(x, g, sh, sc):
    ms = jnp.mean(x * x, axis=-1, keepdims=True)
    return (x * lax.rsqrt(ms + EPS) * g) * (1.0 + sc) + sh


def _mod_kernel(c_ref, w_ref, b_ref, o_ref):
    a = c_ref[...]
    a = a * _sigmoid(a)
    o_ref[0] = jnp.dot(a, w_ref[0], preferred_element_type=F32) + b_ref[0]


def modulation(cond, w_mod, b_mod, tn=1024):
    nl, d, n = w_mod.shape
    return pl.pallas_call(
        _mod_kernel,
        grid=(nl, n // tn),
        in_specs=[pl.BlockSpec((8, d), lambda l, j: (0, 0)),
                  pl.BlockSpec((1, d, tn), lambda l, j: (l, 0, j)),
                  pl.BlockSpec((1, 1, tn), lambda l, j: (l, 0, j))],
        out_specs=pl.BlockSpec((1, 8, tn), lambda l, j: (l, 0, j)),
        out_shape=jax.ShapeDtypeStruct((nl, 8, n), F32),
        compiler_params=_cparams(("parallel", "parallel")),
        name="modulation",
    )(cond, w_mod, b_mod.reshape(nl, 1, n))


def _ln_mm_kernel(x_ref, g_ref, sh_ref, sc_ref, w_ref, o_ref, hn_ref, *, act):
    @pl.when(pl.program_id(1) == 0)
    def _():
        hn_ref[...] = _adaln(x_ref[...], g_ref[...], sh_ref[0], sc_ref[0]).astype(BF16)

    r = jnp.dot(hn_ref[...], w_ref[...], preferred_element_type=F32)
    if act == "gelu":
        r = _gelu(r)
    o_ref[...] = r.astype(o_ref.dtype)


def ln_matmul(x, g, sh, sc, w, rows_per_mod, act=None, tm=512, tn=1024):
    m, d = x.shape
    n = w.shape[1]
    tm = min(tm, m)
    nb = sh.shape[0]
    mod_map = lambda i, j: ((i * tm) // rows_per_mod, 0, 0)
    return pl.pallas_call(
        functools.partial(_ln_mm_kernel, act=act),
        grid=(m // tm, n // tn),
        in_specs=[pl.BlockSpec((tm, d), lambda i, j: (i, 0)),
                  pl.BlockSpec((1, d), lambda i, j: (0, 0)),
                  pl.BlockSpec((1, 1, d), mod_map),
                  pl.BlockSpec((1, 1, d), mod_map),
                  pl.BlockSpec((d, tn), lambda i, j: (0, j))],
        out_specs=pl.BlockSpec((tm, tn), lambda i, j: (i, j)),
        out_shape=jax.ShapeDtypeStruct((m, n), BF16),
        scratch_shapes=[pltpu.VMEM((tm, d), BF16)],
        compiler_params=_cparams(("parallel", "arbitrary")),
        name="ln_matmul",
    )(x, g.reshape(1, d), sh.reshape(nb, 1, d), sc.reshape(nb, 1, d), w)


def _gates_kernel(x_ref, g_ref, sh_ref, sc_ref, w_ref, wt_ref, b_ref, bt_ref, oc_ref, or_ref):
    hn = _adaln(x_ref[...], g_ref[...], sh_ref[0], sc_ref[0])
    oc_ref[...] = jnp.dot(hn, w_ref[...], preferred_element_type=F32) + b_ref[...]
    or_ref[...] = lax.dot_general(wt_ref[...], hn, (((1,), (1,)), ((), ())),
                                  preferred_element_type=F32) + bt_ref[...]


def gate_preacts(x, g, sh, sc, w_g, b_g, rows_per_mod, tm=512):
    m, d = x.shape
    ng = w_g.shape[1]
    tm = min(tm, m)
    nb = sh.shape[0]
    mod_map = lambda i: ((i * tm) // rows_per_mod, 0, 0)
    return pl.pallas_call(
        _gates_kernel,
        grid=(m // tm,),
        in_specs=[pl.BlockSpec((tm, d), lambda i: (i, 0)),
                  pl.BlockSpec((1, d), lambda i: (0, 0)),
                  pl.BlockSpec((1, 1, d), mod_map),
                  pl.BlockSpec((1, 1, d), mod_map),
                  pl.BlockSpec((d, ng), lambda i: (0, 0)),
                  pl.BlockSpec((ng, d), lambda i: (0, 0)),
                  pl.BlockSpec((1, ng), lambda i: (0, 0)),
                  pl.BlockSpec((ng, 1), lambda i: (0, 0))],
        out_specs=[pl.BlockSpec((tm, ng), lambda i: (i, 0)),
                   pl.BlockSpec((ng, tm), lambda i: (0, i))],
        out_shape=[jax.ShapeDtypeStruct((m, ng), F32),
                   jax.ShapeDtypeStruct((ng, m), F32)],
        compiler_params=_cparams(("parallel",)),
        name="gate_preacts",
    )(x, g.reshape(1, d), sh.reshape(nb, 1, d), sc.reshape(nb, 1, d),
      w_g, w_g.T, b_g.reshape(1, ng), b_g.reshape(ng, 1))


def _conv_kernel(*refs, on_grid, tt, nt):
    if on_grid:
        xp_ref, xc_ref, xn_ref, w_ref, b_ref, s_ref, o_ref = refs
    else:
        xc_ref, w_ref, b_ref, s_ref, o_ref = refs
    i = pl.program_id(1)
    ct = xc_ref.shape[-1]
    cur = xc_ref[0].astype(F32)
    tok = lax.broadcasted_iota(jnp.int32, (tt, ct), 0)
    if on_grid:
        first = (tok % GRID_W) == 0
        last = (tok % GRID_W) == GRID_W - 1
    else:
        first = tok == 0
        last = tok == tt - 1

    def row_taps(base, dr):
        left = jnp.where(first, 0.0, pltpu.roll(base, 1, axis=0))
        right = jnp.where(last, 0.0, pltpu.roll(base, tt - 1, axis=0))
        return left * w_ref[dr, 0:1, :] + base * w_ref[dr, 1:2, :] + right * w_ref[dr, 2:3, :]

    y = row_taps(cur, 1)
    if on_grid:
        prev = jnp.where(i > 0, xp_ref[0].astype(F32), 0.0)
        nxt = jnp.where(i < nt - 1, xn_ref[0].astype(F32), 0.0)
        up = jnp.concatenate([prev, cur[:tt - GRID_W]], axis=0)
        down = jnp.concatenate([cur[GRID_W:], nxt], axis=0)
        y = y + row_taps(up, 0) + row_taps(down, 2)
    y = y + b_ref[...]
    y = y * _sigmoid(y)
    o_ref[0] = (y * s_ref[...]).astype(o_ref.dtype)


def conv_silu(proj, conv_w, conv_b, out_scale, n_ch, on_grid, tt=512, ct=512):
    bsz, t, _ = proj.shape
    if not on_grid:
        tt = t
    nt = t // tt
    rpt = tt // GRID_W
    kern = functools.partial(_conv_kernel, on_grid=on_grid, tt=tt, nt=nt)
    cur_spec = pl.BlockSpec((1, tt, ct), lambda b, i, j: (b, i, j))
    par_specs = [pl.BlockSpec((3, 3, ct), lambda b, i, j: (0, 0, j)),
                 pl.BlockSpec((1, ct), lambda b, i, j: (0, j)),
                 pl.BlockSpec((1, ct), lambda b, i, j: (0, j))]
    if on_grid:
        nrow = t // GRID_W
        in_specs = [pl.BlockSpec((1, GRID_W, ct), lambda b, i, j: (b, jnp.maximum(i * rpt - 1, 0), j)),
                    cur_spec,
                    pl.BlockSpec((1, GRID_W, ct), lambda b, i, j: (b, jnp.minimum((i + 1) * rpt, nrow - 1), j))]
        args = (proj, proj, proj)
    else:
        in_specs = [cur_spec]
        args = (proj,)
    return pl.pallas_call(
        kern,
        grid=(bsz, nt, n_ch // ct),
        in_specs=in_specs + par_specs,
        out_specs=pl.BlockSpec((1, tt, ct), lambda b, i, j: (b, i, j)),
        out_shape=jax.ShapeDtypeStruct((bsz, t, n_ch), BF16),
        compiler_params=_cparams(("parallel", "parallel", "parallel")),
        name="conv_silu_grid" if on_grid else "conv_silu_seq",
    )(*args, conv_w, conv_b.reshape(1, n_ch), out_scale.reshape(1, n_ch))


def _mlstm_kernel(*refs, rev, emit, nc):
    if emit:
        (q_ref, k_ref, v_ref, gc_ref, gr_ref, s0_ref, m0_ref,
         h_ref, so_ref, mo_ref, s_scr, m_scr) = refs
    else:
        (k_ref, v_ref, gc_ref, gr_ref, s0_ref, m0_ref,
         so_ref, mo_ref, s_scr, m_scr) = refs
    c = pl.program_id(2)
    ln = ML_CHUNK
    gi = 2 if rev else 0

    @pl.when(c == 0)
    def _():
        s_scr[...] = s0_ref[0, 0]
        m_scr[...] = m0_ref[0, 0]

    gc = gc_ref[0, 0]
    gr = gr_ref[0, 0]
    i_col = gc[:, gi:gi + 1]
    f_col = _log_sigmoid(gc[:, gi + 1:gi + 2])
    i_row = gr[gi:gi + 1, :]
    f_row = _log_sigmoid(gr[gi + 1:gi + 2, :])
    row = lax.broadcasted_iota(jnp.int32, (ln, ln), 0)
    col = lax.broadcasted_iota(jnp.int32, (ln, ln), 1)
    seen = (col >= row) if rev else (col <= row)
    upto = (row >= col) if rev else (row <= col)
    b_col = jnp.sum(jnp.where(seen, f_row, 0.0), axis=1, keepdims=True)
    b_row = jnp.sum(jnp.where(upto, f_col, 0.0), axis=0, keepdims=True)
    b_end = jnp.sum(f_row, axis=1, keepdims=True)
    m_old = m_scr[0:1, 0:1]
    w_log = b_end - b_row + i_row
    m_new = jnp.maximum(b_end + m_old, jnp.max(w_log, axis=1, keepdims=True))
    a_state = jnp.exp(b_end + m_old - m_new)
    w_col = jnp.exp(b_end - b_col + i_col - m_new)

    kc = k_ref[0]
    vc = v_ref[0]
    lane = lax.broadcasted_iota(jnp.int32, (ln, ML_NX), 1)
    ones_col = jnp.where(lane == 0, 1.0, 0.0)
    vx = jnp.concatenate([vc.astype(F32), ones_col], axis=1)
    s_old = s_scr[...]

    if emit:
        qc = q_ref[0]
        inter = b_col + m_old
        d_log = jnp.where(seen, b_col - b_row + i_row, -jnp.inf)
        m_t = jnp.maximum(inter, jnp.max(d_log, axis=1, keepdims=True))
        a_inter = jnp.exp(inter - m_t)
        qk = lax.dot_general(qc, kc, (((1,), (1,)), ((), ())), preferred_element_type=F32)
        p = jnp.exp(d_log - m_t) * qk
        num = a_inter * jnp.dot(qc, s_old.astype(BF16), preferred_element_type=F32) \
            + jnp.dot(p.astype(BF16), vx.astype(BF16), preferred_element_type=F32)
        den = num[:, ML_DV:ML_DV + 1]
        h_ref[0] = num[:, :ML_DV] / jnp.maximum(jnp.abs(den), jnp.exp(-m_t))

    vw = (w_col * vx).astype(BF16)
    kt = kc.astype(F32).T.astype(BF16)
    s_scr[...] = a_state * s_old + jnp.dot(kt, vw, preferred_element_type=F32)
    m_scr[...] = jnp.broadcast_to(m_new, m_scr.shape)

    @pl.when(c == nc - 1)
    def _():
        so_ref[0, 0] = s_scr[...]
        mo_ref[0, 0] = m_scr[...]


def mlstm_scan(qk_act, proj, gcol, grow, s0, m0, rev, emit):
    bsz, t, _ = qk_act.shape
    nc = t // ML_CHUNK
    sw = ML_DV + ML_NX
    cm = (lambda c: nc - 1 - c) if rev else (lambda c: c)
    kq_blocks = ML_HEADS
    v_blocks = (2 * ML_HEADS * ML_DQK) // ML_DV
    in_specs = [
        pl.BlockSpec((1, ML_CHUNK, ML_DQK), lambda b, h, c: (b, cm(c), kq_blocks + h)),
        pl.BlockSpec((1, ML_CHUNK, ML_DV), lambda b, h, c: (b, cm(c), v_blocks + h)),
        pl.BlockSpec((1, 1, ML_CHUNK, 4), lambda b, h, c: (b, h, cm(c), 0)),
        pl.BlockSpec((1, 1, 4, ML_CHUNK), lambda b, h, c: (b, h, 0, cm(c))),
        pl.BlockSpec((1, 1, ML_DQK, sw), lambda b, h, c: (b, h, 0, 0)),
        pl.BlockSpec((1, 1, 8, 128), lambda b, h, c: (b, h, 0, 0)),
    ]
    args = [qk_act, proj, gcol, grow, s0, m0]
    out_specs = [pl.BlockSpec((1, 1, ML_DQK, sw), lambda b, h, c: (b, h, 0, 0)),
                 pl.BlockSpec((1, 1, 8, 128), lambda b, h, c: (b, h, 0, 0))]
    out_shape = [jax.ShapeDtypeStruct((bsz, ML_HEADS, ML_DQK, sw), F32),
                 jax.ShapeDtypeStruct((bsz, ML_HEADS, 8, 128), F32)]
    if emit:
        in_specs = [pl.BlockSpec((1, ML_CHUNK, ML_DQK), lambda b, h, c: (b, cm(c), h))] + in_specs
        args = [qk_act] + args
        out_specs = [pl.BlockSpec((1, ML_CHUNK, ML_DV), lambda b, h, c: (b, cm(c), h))] + out_specs
        out_shape = [jax.ShapeDtypeStruct((bsz, t, ML_HEADS * ML_DV), F32)] + out_shape
    outs = pl.pallas_call(
        functools.partial(_mlstm_kernel, rev=rev, emit=emit, nc=nc),
        grid=(bsz, ML_HEADS, nc),
        in_specs=in_specs,
        out_specs=out_specs,
        out_shape=out_shape,
        scratch_shapes=[pltpu.VMEM((ML_DQK, sw), F32), pltpu.VMEM((8, 128), F32)],
        compiler_params=_cparams(("parallel", "parallel", "arbitrary")),
        name="mlstm_" + ("bwd" if rev else "fwd") + ("_emit" if emit else "_state"),
    )(*args)
    if emit:
        return outs[0], outs[1], outs[2]
    return None, outs[0], outs[1]


def _ml_out_kernel(hf_ref, hb_ref, o_ref, hg_ref, w_ref, res_ref, gt_ref, out_ref, y_ref):
    @pl.when(pl.program_id(1) == 0)
    def _():
        h = hf_ref[...] + hb_ref[...]
        parts = []
        for hd in range(ML_HEADS):
            seg = h[:, hd * ML_DV:(hd + 1) * ML_DV]
            ms = jnp.mean(seg * seg, axis=-1, keepdims=True)
            parts.append(seg * lax.rsqrt(ms + EPS))
        hn = jnp.concatenate(parts, axis=1) * hg_ref[...]
        y_ref[...] = (hn * _sigmoid(o_ref[...].astype(F32))).astype(BF16)

    y = jnp.dot(y_ref[...], w_ref[...], preferred_element_type=F32)
    out_ref[...] = res_ref[...] + gt_ref[0] * y


def mlstm_out(h_f, h_b, proj, head_g, w_out, res, gt, rows_per_mod, tm=512, tn=1024):
    m, d = res.shape
    hw = h_f.shape[1]
    o_block = proj.shape[1] // hw - 1
    nb = gt.shape[0]
    mod_map = lambda i, j: ((i * tm) // rows_per_mod, 0, j)
    return pl.pallas_call(
        _ml_out_kernel,
        grid=(m // tm, d // tn),
        in_specs=[pl.BlockSpec((tm, hw), lambda i, j: (i, 0)),
                  pl.BlockSpec((tm, hw), lambda i, j: (i, 0)),
                  pl.BlockSpec((tm, hw), lambda i, j: (i, o_block)),
                  pl.BlockSpec((1, hw), lambda i, j: (0, 0)),
                  pl.BlockSpec((hw, tn), lambda i, j: (0, j)),
                  pl.BlockSpec((tm, tn), lambda i, j: (i, j)),
                  pl.BlockSpec((1, 1, tn), mod_map)],
        out_specs=pl.BlockSpec((tm, tn), lambda i, j: (i, j)),
        out_shape=jax.ShapeDtypeStruct((m, d), F32),
        scratch_shapes=[pltpu.VMEM((tm, hw), BF16)],
        compiler_params=_cparams(("parallel", "arbitrary")),
        name="mlstm_out",
    )(h_f, h_b, proj, head_g.reshape(1, hw), w_out, res, gt.reshape(nb, 1, d))


def _cm_out_kernel(z_ref, gv_ref, ws_ref, bs_ref, w_ref, res_ref, gt_ref, out_ref, y_ref, *, tm):
    width = gv_ref.shape[-1]
    gdim = width // CM_GROUPS

    @pl.when(pl.program_id(1) == 0)
    def _():
        for ch in range(tm // CM_CHUNK):
            rows = slice(ch * CM_CHUNK, (ch + 1) * CM_CHUNK)
            v = z_ref[rows, width:].astype(F32)
            v = v * lax.rsqrt(jnp.mean(v * v, axis=-1, keepdims=True) + EPS) * gv_ref[...]
            for g in range(CM_GROUPS):
                cols = slice(g * gdim, (g + 1) * gdim)
                sv = jnp.dot(ws_ref[g], v[:, cols].astype(BF16), preferred_element_type=F32) \
                    + bs_ref[:, g:g + 1]
                y_ref[rows, cols] = (z_ref[rows, cols].astype(F32) * sv).astype(BF16)

    y = jnp.dot(y_ref[...], w_ref[...], preferred_element_type=F32)
    out_ref[...] = res_ref[...] + gt_ref[0] * y


def chunk_mlp_out(z, g_v, w_s, b_s, w_out, res, gt, rows_per_mod, tm=512, tn=1024):
    m, d = res.shape
    width = g_v.shape[0]
    nb = gt.shape[0]
    mod_map = lambda i, j: ((i * tm) // rows_per_mod, 0, j)
    return pl.pallas_call(
        functools.partial(_cm_out_kernel, tm=tm),
        grid=(m // tm, d // tn),
        in_specs=[pl.BlockSpec((tm, 2 * width), lambda i, j: (i, 0)),
                  pl.BlockSpec((1, width), lambda i, j: (0, 0)),
                  pl.BlockSpec((CM_GROUPS, CM_CHUNK, CM_CHUNK), lambda i, j: (0, 0, 0)),
                  pl.BlockSpec((CM_CHUNK, CM_GROUPS), lambda i, j: (0, 0)),
                  pl.BlockSpec((width, tn), lambda i, j: (0, j)),
                  pl.BlockSpec((tm, tn), lambda i, j: (i, j)),
                  pl.BlockSpec((1, 1, tn), mod_map)],
        out_specs=pl.BlockSpec((tm, tn), lambda i, j: (i, j)),
        out_shape=jax.ShapeDtypeStruct((m, d), F32),
        scratch_shapes=[pltpu.VMEM((tm, width), BF16)],
        compiler_params=_cparams(("parallel", "arbitrary")),
        name="chunk_mlp_out",
    )(z, g_v.reshape(1, width), w_s, b_s.T, w_out, res, gt.reshape(nb, 1, d))


def _extract_top(s, payload, n_out):
    nrows = s.shape[0]
    rid = lax.broadcasted_iota(jnp.int32, s.shape, 0)
    vals, pays = [], []
    for _ in range(n_out):
        mx = jnp.max(s, axis=0, keepdims=True)
        pos = jnp.min(jnp.where(s == mx, rid, nrows), axis=0, keepdims=True)
        hit = rid == pos
        vals.append(mx)
        if payload is None:
            pays.append(pos)
        else:
            pays.append(jnp.sum(jnp.where(hit, payload, 0), axis=0, keepdims=True))
        s = jnp.where(hit, -jnp.inf, s)
    return jnp.concatenate(vals, axis=0), jnp.concatenate(pays, axis=0)


def _pair_candidates(s1, i1, s2, i2):
    k = PEER_TOPK
    sub = lax.broadcasted_iota(jnp.int32, (8, s1.shape[1]), 0)
    cand, cidx = [], []
    for bt in range(k // 8):
        cand.append(s1[0:1, :] + s2[bt * 8:(bt + 1) * 8, :])
        cidx.append(i1[0:1, :] * PEER_NKEYS + i2[bt * 8:(bt + 1) * 8, :])
    for a in range(1, 8):
        bmax = k // (a + 1) - 1
        c = s1[a:a + 1, :] + s2[0:8, :]
        cand.append(c if bmax >= 7 else jnp.where(sub <= bmax, c, -jnp.inf))
        cidx.append(i1[a:a + 1, :] * PEER_NKEYS + i2[0:8, :])
    cand.append(s1[8:k, :] + s2[0:1, :])
    cidx.append(i1[8:k, :] * PEER_NKEYS + i2[0:1, :])
    return jnp.concatenate(cand, axis=0), jnp.concatenate(cidx, axis=0)


def _peer_topk_kernel(h_ref, g_ref, sh_ref, sc_ref, wq_ref, keys_ref, hn_ref, idx_ref, gate_ref):
    hn = _adaln(h_ref[...], g_ref[...], sh_ref[0], sc_ref[0])
    hn_ref[...] = hn
    qt = lax.dot_general(wq_ref[...], hn.astype(BF16), (((1,), (1,)), ((), ())),
                         preferred_element_type=F32)
    idx_rows, gate_rows = [], []
    for hd in range(PEER_HEADS):
        top = []
        for p in range(2):
            hp = hd * 2 + p
            s = jnp.dot(keys_ref[hp], qt[hp * PEER_HALF:(hp + 1) * PEER_HALF, :],
                        preferred_element_type=F32, precision=HI)
            top.append(_extract_top(s, None, PEER_TOPK))
        (s1, i1), (s2, i2) = top
        cand, cidx = _pair_candidates(s1, i1, s2, i2)
        top_s, eidx = _extract_top(cand, cidx, PEER_TOPK)
        e = jnp.exp(top_s - top_s[0:1, :])
        gate_rows.append(e / jnp.sum(e, axis=0, keepdims=True))
        idx_rows.append(eidx)
    idx_ref[...] = jnp.concatenate(idx_rows, axis=0).T
    gate_ref[...] = jnp.concatenate(gate_rows, axis=0).T


def peer_topk(h, g, sh, sc, wq_t, keys, rows_per_mod, tm=256):
    m, d = h.shape
    tm = min(tm, m)
    nb = sh.shape[0]
    mod_map = lambda i: ((i * tm) // rows_per_mod, 0, 0)
    nq = wq_t.shape[0]
    return pl.pallas_call(
        _peer_topk_kernel,
        grid=(m // tm,),
        in_specs=[pl.BlockSpec((tm, d), lambda i: (i, 0)),
                  pl.BlockSpec((1, d), lambda i: (0, 0)),
                  pl.BlockSpec((1, 1, d), mod_map),
                  pl.BlockSpec((1, 1, d), mod_map),
                  pl.BlockSpec((nq, d), lambda i: (0, 0)),
                  pl.BlockSpec(keys.shape, lambda i: (0, 0, 0))],
        out_specs=[pl.BlockSpec((tm, d), lambda i: (i, 0)),
                   pl.BlockSpec((tm, PEER_SEL), lambda i: (i, 0)),
                   pl.BlockSpec((tm, PEER_SEL), lambda i: (i, 0))],
        out_shape=[jax.ShapeDtypeStruct((m, d), F32),
                   jax.ShapeDtypeStruct((m, PEER_SEL), jnp.int32),
                   jax.ShapeDtypeStruct((m, PEER_SEL), F32)],
        compiler_params=_cparams(("parallel",)),
        name="peer_topk",
    )(h, g.reshape(1, d), sh.reshape(nb, 1, d), sc.reshape(nb, 1, d), wq_t, keys)


PEER_SLOTS = 5
PEER_AHEAD = PEER_SLOTS - 1


def _sublane_sums(vs):
    sub = lax.broadcasted_iota(jnp.int32, (8, 128), 0)
    lo4 = sub < 4
    w = []
    for j in range(4):
        a = jnp.where(lo4, vs[j], vs[j + 4])
        b = jnp.where(lo4, vs[j + 4], vs[j])
        w.append(a + pltpu.roll(b, 4, axis=0))
    lo2 = (sub % 4) < 2
    x = []
    for j in range(2):
        t1 = w[j] + pltpu.roll(w[j], 6, axis=0)
        t2 = w[j + 2] + pltpu.roll(w[j + 2], 2, axis=0)
        x.append(jnp.where(lo2, t1, t2))
    even = (sub % 2) == 0
    t1 = x[0] + pltpu.roll(x[0], 7, axis=0)
    t2 = x[1] + pltpu.roll(x[1], 1, axis=0)
    return jnp.where(even, t1, t2)


def _unpack_pair(w):
    lo = lax.bitcast_convert_type(lax.shift_left(w, jnp.int32(16)), F32)
    hi = lax.bitcast_convert_type(lax.bitwise_and(w, jnp.int32(-65536)), F32)
    return lo, hi


def pack_expert_table(u_emb, v_emb):
    ne, d = u_emb.shape
    npk = d // 2048

    def pack(w):
        h = lax.bitcast_convert_type(w.astype(BF16), jnp.uint16).astype(jnp.uint32).reshape(ne, 2, npk * 8, 128)
        return lax.bitcast_convert_type(h[:, 0] | (h[:, 1] << 16), jnp.int32)

    return jnp.concatenate([pack(u_emb), pack(v_emb)], axis=1)


def _peer_gather_kernel(idx_hbm, gate_ref, x_ref, res_ref, gt_ref, tab_hbm, out_ref,
                        idx_smem, buf, sems, idx_sem, act_ref, zred_ref, *, tb):
    nrow = x_ref.shape[1]
    nvr = nrow // 8
    npk = nvr // 2
    blk = pl.program_id(0)

    idx_cp = pltpu.make_async_copy(idx_hbm.at[pl.ds(blk * (tb * PEER_SEL), tb * PEER_SEL)], idx_smem, idx_sem)
    idx_cp.start()
    idx_cp.wait()

    def start_row(base, slot, k):
        pltpu.make_async_copy(tab_hbm.at[idx_smem[base + k]], buf.at[slot, k], sems.at[slot]).start()

    def issue(t, slot):
        for k in range(PEER_SEL):
            start_row(t * PEER_SEL, slot, k)

    def issuer(t, slot):
        nxt = [0]

        def pump(n):
            for _ in range(n):
                if nxt[0] < PEER_SEL:
                    start_row(t * PEER_SEL, slot, nxt[0])
                    nxt[0] += 1
        return pump

    def wait(slot):
        pltpu.make_async_copy(tab_hbm.at[pl.ds(0, PEER_SEL)], buf.at[slot], sems.at[slot]).wait()

    last = tb - 1
    no_pump = lambda n: None

    def reduce_phase(t, slot, zslot, pump):
        xs = [x_ref[t, r * 8:(r + 1) * 8, :] for r in range(nvr)]
        for g8 in range(PEER_SEL // 8):
            ps = []
            for k in range(g8 * 8, g8 * 8 + 8):
                p = None
                for j in range(npk):
                    lo, hi = _unpack_pair(buf[slot, k, j * 8:(j + 1) * 8, :])
                    q = lo * xs[j] + hi * xs[j + npk]
                    p = q if p is None else p + q
                ps.append(p)
                if k % 8 in (1, 4, 7):
                    pump(1)
            zred_ref[zslot, g8 * 8:(g8 + 1) * 8, :] = _sublane_sums(ps)

    def act_values(t, zslot):
        ones = jnp.ones((128, 128), F32)
        diag = (lax.broadcasted_iota(jnp.int32, (128, 128), 0)
                == lax.broadcasted_iota(jnp.int32, (128, 128), 1))
        score = jnp.dot(zred_ref[zslot], ones, preferred_element_type=F32, precision=HI)
        gate = jnp.dot(jnp.where(diag, gate_ref[t], 0.0), ones, preferred_element_type=F32, precision=HI)
        return _gelu(score) * gate

    def mix_phase(t, slot, aslot, pump):
        nacc = 4
        accs = [[None] * nvr for _ in range(nacc)]
        for k in range(PEER_SEL):
            a = jnp.broadcast_to(act_ref[aslot, k:k + 1, :], (8, 128))
            for j in range(npk):
                pair = _unpack_pair(buf[slot, k, (npk + j) * 8:(npk + j + 1) * 8, :])
                for r, val in ((j, pair[0]), (j + npk, pair[1])):
                    term = a * val
                    prev = accs[k % nacc][r]
                    accs[k % nacc][r] = term if prev is None else prev + term
            if k % 2 == 0 and k < 112:
                pump(1)
        for r in range(nvr):
            tot = accs[0][r]
            for j in range(1, nacc):
                tot = tot + accs[j][r]
            rows = slice(r * 8, (r + 1) * 8)
            out_ref[t, rows, :] = res_ref[t, rows, :] + gt_ref[0, rows, :] * tot

    for t0 in range(PEER_AHEAD):
        issue(t0, t0)
    wait(0)
    reduce_phase(0, 0, 0, no_pump)
    act_ref[0] = act_values(0, 0)
    wait(1)
    reduce_phase(1, 1, 1, no_pump)

    def trip(t, carry):
        ta = t + PEER_AHEAD
        pump = issuer(jnp.minimum(ta, last), ta % PEER_SLOTS)
        t1 = t + 1
        act_next = act_values(jnp.minimum(t1, last), t1 % 2)
        mix_phase(t, t % PEER_SLOTS, t % 2, pump)
        t2 = t + 2
        wait(t2 % PEER_SLOTS)
        reduce_phase(jnp.minimum(t2, last), t2 % PEER_SLOTS, t2 % 2, pump)
        pump(PEER_SEL)
        act_ref[t1 % 2] = act_next
        return carry

    lax.fori_loop(0, tb, trip, 0)
    for e in range(2, PEER_AHEAD):
        wait((tb + e) % PEER_SLOTS)


def peer_gather(idx, gate, x, res, gt, table, rows_per_mod, tb=256):
    m, d = x.shape
    tb = min(tb, m)
    nb = gt.shape[0]
    nrow = d // 128
    mod_map = lambda i: ((i * tb) // rows_per_mod, 0, 0)
    tok_spec = pl.BlockSpec((tb, nrow, 128), lambda i: (i, 0, 0))
    out = pl.pallas_call(
        functools.partial(_peer_gather_kernel, tb=tb),
        grid=(m // tb,),
        in_specs=[pl.BlockSpec(memory_space=pl.ANY),
                  pl.BlockSpec((tb, 1, PEER_SEL), lambda i: (i, 0, 0)),
                  tok_spec,
                  tok_spec,
                  pl.BlockSpec((1, nrow, 128), mod_map),
                  pl.BlockSpec(memory_space=pl.ANY)],
        out_specs=tok_spec,
        out_shape=jax.ShapeDtypeStruct((m, nrow, 128), F32),
        scratch_shapes=[pltpu.SMEM((tb * PEER_SEL,), jnp.int32),
                        pltpu.VMEM((PEER_SLOTS, PEER_SEL, nrow, 128), jnp.int32),
                        pltpu.SemaphoreType.DMA((PEER_SLOTS,)),
                        pltpu.SemaphoreType.DMA(()),
                        pltpu.VMEM((2, PEER_SEL, 128), F32),
                        pltpu.VMEM((2, PEER_SEL, 128), F32)],
        compiler_params=_cparams(("arbitrary",)),
        name="peer_gather",
    )(idx.reshape(-1), gate.reshape(m, 1, PEER_SEL), x.reshape(m, nrow, 128), res.reshape(m, nrow, 128),
      gt.reshape(nb, nrow, 128), table)
    return out.reshape(m, d)


def _rmsnorm_kernel(x_ref, g_ref, o_ref):
    x = x_ref[...]
    o_ref[...] = x * lax.rsqrt(jnp.mean(x * x, axis=-1, keepdims=True) + EPS) * g_ref[...]


def rmsnorm(x, g, tm=512):
    m, d = x.shape
    return pl.pallas_call(
        _rmsnorm_kernel,
        grid=(m // tm,),
        in_specs=[pl.BlockSpec((tm, d), lambda i: (i, 0)), pl.BlockSpec((1, d), lambda i: (0, 0))],
        out_specs=pl.BlockSpec((tm, d), lambda i: (i, 0)),
        out_shape=jax.ShapeDtypeStruct((m, d), F32),
        compiler_params=_cparams(("parallel",)),
        name="final_rmsnorm",
    )(x, g.reshape(1, d))


def _split_gates(gcol, grow, bsz, t):
    gc = gcol.reshape(bsz, t, 4, ML_HEADS).transpose(0, 3, 1, 2)
    gr = grow.reshape(4, ML_HEADS, bsz, t).transpose(2, 1, 0, 3)
    return gc, gr


def _mlstm_layer(h_lat, h_ctx, mods, cmods, g_mix, w_in, conv_w, conv_b, gate_b, head_g, w_out):
    bsz, t, d = h_lat.shape
    tc = h_ctx.shape[1]
    sh1, sc1, gt1 = mods
    csh1, csc1 = cmods
    qk_w = 2 * ML_HEADS * ML_DQK
    n_main = qk_w + 2 * ML_HEADS * ML_DV
    w_main = w_in[:, :n_main].astype(BF16)
    w_g = w_in[:, n_main:]
    k_scale = jnp.concatenate([jnp.ones((qk_w // 2,), F32), jnp.full((qk_w // 2,), ML_DQK ** -0.5, F32)])
    sw = ML_DV + ML_NX

    xc = h_ctx.reshape(bsz * tc, d)
    projc = ln_matmul(xc, g_mix, csh1, csc1, w_main, bsz * tc).reshape(bsz, tc, n_main)
    gcc, grc = gate_preacts(xc, g_mix, csh1, csc1, w_g, gate_b, bsz * tc)
    gcc, grc = _split_gates(gcc, grc, bsz, tc)
    qkc = conv_silu(projc, conv_w, conv_b, k_scale, qk_w, on_grid=False)
    s0 = jnp.zeros((bsz, ML_HEADS, ML_DQK, sw), F32)
    m0 = jnp.zeros((bsz, ML_HEADS, 8, 128), F32)
    _, sf, mf = mlstm_scan(qkc, projc, gcc, grc, s0, m0, rev=False, emit=False)
    _, sb, mb = mlstm_scan(qkc, projc, gcc, grc, s0, m0, rev=True, emit=False)

    x2 = h_lat.reshape(bsz * t, d)
    proj = ln_matmul(x2, g_mix, sh1, sc1, w_main, t).reshape(bsz, t, n_main)
    gcl, grl = gate_preacts(x2, g_mix, sh1, sc1, w_g, gate_b, t)
    gcl, grl = _split_gates(gcl, grl, bsz, t)
    qk = conv_silu(proj, conv_w, conv_b, k_scale, qk_w, on_grid=True)
    h_f, _, _ = mlstm_scan(qk, proj, gcl, grl, sf, mf, rev=False, emit=True)
    h_b, _, _ = mlstm_scan(qk, proj, gcl, grl, sb, mb, rev=True, emit=True)
    hw = ML_HEADS * ML_DV
    out = mlstm_out(h_f.reshape(bsz * t, hw), h_b.reshape(bsz * t, hw), proj.reshape(bsz * t, n_main),
                    head_g.reshape(-1), w_out.astype(BF16), x2, gt1, t)
    return out


def _chunk_mlp_layer(x2, t, mods, g_mix, w_in, g_v, w_s, b_s, w_out):
    sh1, sc1, gt1 = mods
    z = ln_matmul(x2, g_mix, sh1, sc1, w_in.astype(BF16), t, act="gelu")
    return chunk_mlp_out(z, g_v, w_s.astype(BF16), b_s, w_out.astype(BF16), x2, gt1, t)


def _peer_layer(x2, t, mods, g_chan, w_q, keys, u_emb, v_emb):
    sh2, sc2, gt2 = mods
    d = x2.shape[1]
    hn, idx, gate = peer_topk(x2, g_chan, sh2, sc2, w_q.T.astype(BF16),
                              keys.reshape(PEER_HEADS * 2, PEER_NKEYS, PEER_HALF), t)
    return peer_gather(idx, gate, hn, x2, gt2, pack_expert_table(u_emb, v_emb), t)


def kernel(x, c, ctx, c_ctx, w_mod, b_mod, g_mix, g_chan, ml_w_in, ml_conv_w, ml_conv_b, ml_gate_b, ml_head_g, ml_w_out, cm_w_in, cm_g_v, cm_w_s, cm_b_s, cm_w_out, peer_w_q, peer_keys, peer_u, peer_v, g_final):
    bsz, t, d = x.shape
    depth = w_mod.shape[0]
    cond = jnp.zeros((8, d), F32).at[:bsz].set(c).at[bsz].set(c_ctx)
    mods = modulation(cond, w_mod, b_mod)
    h2 = x.reshape(bsz * t, d)
    for i in range(depth):
        j = i // 2
        mi = mods[i].reshape(8, 6, d)
        lat = [mi[:bsz, s] for s in range(6)]
        cx = [mi[bsz:bsz + 1, s] for s in range(6)]
        if i % 2 == 0:
            h2 = _mlstm_layer(h2.reshape(bsz, t, d), ctx, lat[:3], cx[:2], g_mix[i], ml_w_in[j], ml_conv_w[j],
                              ml_conv_b[j], ml_gate_b[j], ml_head_g[j], ml_w_out[j])
        else:
            h2 = _chunk_mlp_layer(h2, t, lat[:3], g_mix[i], cm_w_in[j], cm_g_v[j], cm_w_s[j], cm_b_s[j],
                                  cm_w_out[j])
        h2 = _peer_layer(h2, t, lat[3:], g_chan[i], peer_w_q[i], peer_keys[i], peer_u[i], peer_v[i])
    return rmsnorm(h2, g_final).reshape(bsz, t, d)
```

```python
import functools

import jax
import jax.numpy as jnp
from jax import lax
from jax.experimental import pallas as pl
from jax.experimental.pallas import tpu as pltpu

F32 = jnp.float32
BF16 = jnp.bfloat16
EPS = 1e-6

GRID_W = 64
ML_HEADS = 4
ML_DQK = 256
ML_DV = 512
ML_CHUNK = 128
ML_NX = 128
CM_CHUNK = 128
CM_GROUPS = 8
PEER_HEADS = 8
PEER_NKEYS = 128
PEER_HALF = 128
PEER_TOPK = 16
PEER_SEL = PEER_HEADS * PEER_TOPK

VMEM_LIMIT = 48 * 1024 * 1024
HI = lax.Precision.HIGHEST


def _cparams(sem):
    return pltpu.CompilerParams(dimension_semantics=sem, vmem_limit_bytes=VMEM_LIMIT)


def _sigmoid(x):
    return 1.0 / (1.0 + jnp.exp(-x))


def _gelu(x):
    return 0.5 * x * (1.0 + jnp.tanh(0.7978845608028654 * (x + 0.044715 * (x * x * x))))


def _log_sigmoid(x):
    return jnp.minimum(x, 0.0) - jnp.log(1.0 + jnp.exp(-jnp.abs(x)))


def _adaln(x, g, sh, sc):
    ms = jnp.mean(x * x, axis=-1, keepdims=True)
    return (x * lax.rsqrt(ms + EPS) * g) * (1.0 + sc) + sh


def _mod_kernel(c_ref, w_ref, b_ref, o_ref):
    a = c_ref[...]
    a = a * _sigmoid(a)
    o_ref[0] = jnp.dot(a, w_ref[0], preferred_element_type=F32) + b_ref[0]


def modulation(cond, w_mod, b_mod, tn=1024):
    nl, d, n = w_mod.shape
    return pl.pallas_call(
        _mod_kernel,
        grid=(nl, n // tn),
        in_specs=[pl.BlockSpec((8, d), lambda l, j: (0, 0)),
                  pl.BlockSpec((1, d, tn), lambda l, j: (l, 0, j)),
                  pl.BlockSpec((1, 1, tn), lambda l, j: (l, 0, j))],
        out_specs=pl.BlockSpec((1, 8, tn), lambda l, j: (l, 0, j)),
        out_shape=jax.ShapeDtypeStruct((nl, 8, n), F32),
        compiler_params=_cparams(("parallel", "parallel")),
        name="modulation",
    )(cond, w_mod, b_mod.reshape(nl, 1, n))


def _ln_mm_kernel(x_ref, g_ref, sh_ref, sc_ref, w_ref, o_ref, hn_ref, *, act):
    @pl.when(pl.program_id(1) == 0)
    def _():
        hn_ref[...] = _adaln(x_ref[...], g_ref[...], sh_ref[0], sc_ref[0]).astype(BF16)

    r = jnp.dot(hn_ref[...], w_ref[...], preferred_element_type=F32)
    if act == "gelu":
        r = _gelu(r)
    o_ref[...] = r.astype(o_ref.dtype)


def ln_matmul(x, g, sh, sc, w, rows_per_mod, act=None, tm=512, tn=1024):
    m, d = x.shape
    n = w.shape[1]
    tm = min(tm, m)
    nb = sh.shape[0]
    mod_map = lambda i, j: ((i * tm) // rows_per_mod, 0, 0)
    return pl.pallas_call(
        functools.partial(_ln_mm_kernel, act=act),
        grid=(m // tm, n // tn),
        in_specs=[pl.BlockSpec((tm, d), lambda i, j: (i, 0)),
                  pl.BlockSpec((1, d), lambda i, j: (0, 0)),
                  pl.BlockSpec((1, 1, d), mod_map),
                  pl.BlockSpec((1, 1, d), mod_map),
                  pl.BlockSpec((d, tn), lambda i, j: (0, j))],
        out_specs=pl.BlockSpec((tm, tn), lambda i, j: (i, j)),
        out_shape=jax.ShapeDtypeStruct((m, n), BF16),
        scratch_shapes=[pltpu.VMEM((tm, d), BF16)],
        compiler_params=_cparams(("parallel", "arbitrary")),
        name="ln_matmul",
    )(x, g.reshape(1, d), sh.reshape(nb, 1, d), sc.reshape(nb, 1, d), w)


def _gates_kernel(x_ref, g_ref, sh_ref, sc_ref, w_ref, wt_ref, b_ref, bt_ref, oc_ref, or_ref):
    hn = _adaln(x_ref[...], g_ref[...], sh_ref[0], sc_ref[0])
    oc_ref[...] = jnp.dot(hn, w_ref[...], preferred_element_type=F32) + b_ref[...]
    or_ref[...] = lax.dot_general(wt_ref[...], hn, (((1,), (1,)), ((), ())),
                                  preferred_element_type=F32) + bt_ref[...]


def gate_preacts(x, g, sh, sc, w_g, b_g, rows_per_mod, tm=512):
    m, d = x.shape
    ng = w_g.shape[1]
    tm = min(tm, m)
    nb = sh.shape[0]
    mod_map = lambda i: ((i * tm) // rows_per_mod, 0, 0)
    return pl.pallas_call(
        _gates_kernel,
        grid=(m // tm,),
        in_specs=[pl.BlockSpec((tm, d), lambda i: (i, 0)),
                  pl.BlockSpec((1, d), lambda i: (0, 0)),
                  pl.BlockSpec((1, 1, d), mod_map),
                  pl.BlockSpec((1, 1, d), mod_map),
                  pl.BlockSpec((d, ng), lambda i: (0, 0)),
                  pl.BlockSpec((ng, d), lambda i: (0, 0)),
                  pl.BlockSpec((1, ng), lambda i: (0, 0)),
                  pl.BlockSpec((ng, 1), lambda i: (0, 0))],
        out_specs=[pl.BlockSpec((tm, ng), lambda i: (i, 0)),
                   pl.BlockSpec((ng, tm), lambda i: (0, i))],
        out_shape=[jax.ShapeDtypeStruct((m, ng), F32),
                   jax.ShapeDtypeStruct((ng, m), F32)],
        compiler_params=_cparams(("parallel",)),
        name="gate_preacts",
    )(x, g.reshape(1, d), sh.reshape(nb, 1, d), sc.reshape(nb, 1, d),
      w_g, w_g.T, b_g.reshape(1, ng), b_g.reshape(ng, 1))


def _conv_kernel(*refs, on_grid, tt, nt):
    if on_grid:
        xp_ref, xc_ref, xn_ref, w_ref, b_ref, s_ref, o_ref = refs
    else:
        xc_ref, w_ref, b_ref, s_ref, o_ref = refs
    i = pl.program_id(1)
    ct = xc_ref.shape[-1]
    cur = xc_ref[0].astype(F32)
    tok = lax.broadcasted_iota(jnp.int32, (tt, ct), 0)
    if on_grid:
        first = (tok % GRID_W) == 0
        last = (tok % GRID_W) == GRID_W - 1
    else:
        first = tok == 0
        last = tok == tt - 1

    def row_taps(base, dr):
        left = jnp.where(first, 0.0, pltpu.roll(base, 1, axis=0))
        right = jnp.where(last, 0.0, pltpu.roll(base, tt - 1, axis=0))
        return left * w_ref[dr, 0:1, :] + base * w_ref[dr, 1:2, :] + right * w_ref[dr, 2:3, :]

    y = row_taps(cur, 1)
    if on_grid:
        prev = jnp.where(i > 0, xp_ref[0].astype(F32), 0.0)
        nxt = jnp.where(i < nt - 1, xn_ref[0].astype(F32), 0.0)
        up = jnp.concatenate([prev, cur[:tt - GRID_W]], axis=0)
        down = jnp.concatenate([cur[GRID_W:], nxt], axis=0)
        y = y + row_taps(up, 0) + row_taps(down, 2)
    y = y + b_ref[...]
    y = y * _sigmoid(y)
    o_ref[0] = (y * s_ref[...]).astype(o_ref.dtype)


def conv_silu(proj, conv_w, conv_b, out_scale, n_ch, on_grid, tt=512, ct=512):
    bsz, t, _ = proj.shape
    if not on_grid:
        tt = t
    nt = t // tt
    rpt = tt // GRID_W
    kern = functools.partial(_conv_kernel, on_grid=on_grid, tt=tt, nt=nt)
    cur_spec = pl.BlockSpec((1, tt, ct), lambda b, i, j: (b, i, j))
    par_specs = [pl.BlockSpec((3, 3, ct), lambda b, i, j: (0, 0, j)),
                 pl.BlockSpec((1, ct), lambda b, i, j: (0, j)),
                 pl.BlockSpec((1, ct), lambda b, i, j: (0, j))]
    if on_grid:
        nrow = t // GRID_W
        in_specs = [pl.BlockSpec((1, GRID_W, ct), lambda b, i, j: (b, jnp.maximum(i * rpt - 1, 0), j)),
                    cur_spec,
                    pl.BlockSpec((1, GRID_W, ct), lambda b, i, j: (b, jnp.minimum((i + 1) * rpt, nrow - 1), j))]
        args = (proj, proj, proj)
    else:
        in_specs = [cur_spec]
        args = (proj,)
    return pl.pallas_call(
        kern,
        grid=(bsz, nt, n_ch // ct),
        in_specs=in_specs + par_specs,
        out_specs=pl.BlockSpec((1, tt, ct), lambda b, i, j: (b, i, j)),
        out_shape=jax.ShapeDtypeStruct((bsz, t, n_ch), BF16),
        compiler_params=_cparams(("parallel", "parallel", "parallel")),
        name="conv_silu_grid" if on_grid else "conv_silu_seq",
    )(*args, conv_w, conv_b.reshape(1, n_ch), out_scale.reshape(1, n_ch))


def _mlstm_kernel(*refs, rev, emit, nc):
    if emit:
        (q_ref, k_ref, v_ref, gc_ref, gr_ref, s0_ref, m0_ref,
         h_ref, so_ref, mo_ref, s_scr, m_scr) = refs
    else:
        (k_ref, v_ref, gc_ref, gr_ref, s0_ref, m0_ref,
         so_ref, mo_ref, s_scr, m_scr) = refs
    c = pl.program_id(2)
    ln = ML_CHUNK
    gi = 2 if rev else 0

    @pl.when(c == 0)
    def _():
        s_scr[...] = s0_ref[0, 0]
        m_scr[...] = m0_ref[0, 0]

    gc = gc_ref[0, 0]
    gr = gr_ref[0, 0]
    i_col = gc[:, gi:gi + 1]
    f_col = _log_sigmoid(gc[:, gi + 1:gi + 2])
    i_row = gr[gi:gi + 1, :]
    f_row = _log_sigmoid(gr[gi + 1:gi + 2, :])
    row = lax.broadcasted_iota(jnp.int32, (ln, ln), 0)
    col = lax.broadcasted_iota(jnp.int32, (ln, ln), 1)
    seen = (col >= row) if rev else (col <= row)
    upto = (row >= col) if rev else (row <= col)
    b_col = jnp.sum(jnp.where(seen, f_row, 0.0), axis=1, keepdims=True)
    b_row = jnp.sum(jnp.where(upto, f_col, 0.0), axis=0, keepdims=True)
    b_end = jnp.sum(f_row, axis=1, keepdims=True)
    m_old = m_scr[0:1, 0:1]
    w_log = b_end - b_row + i_row
    m_new = jnp.maximum(b_end + m_old, jnp.max(w_log, axis=1, keepdims=True))
    a_state = jnp.exp(b_end + m_old - m_new)
    w_col = jnp.exp(b_end - b_col + i_col - m_new)

    kc = k_ref[0]
    vc = v_ref[0]
    lane = lax.broadcasted_iota(jnp.int32, (ln, ML_NX), 1)
    ones_col = jnp.where(lane == 0, 1.0, 0.0)
    vx = jnp.concatenate([vc.astype(F32), ones_col], axis=1)
    s_old = s_scr[...]

    if emit:
        qc = q_ref[0]
        inter = b_col + m_old
        d_log = jnp.where(seen, b_col - b_row + i_row, -jnp.inf)
        m_t = jnp.maximum(inter, jnp.max(d_log, axis=1, keepdims=True))
        a_inter = jnp.exp(inter - m_t)
        qk = lax.dot_general(qc, kc, (((1,), (1,)), ((), ())), preferred_element_type=F32)
        p = jnp.exp(d_log - m_t) * qk
        num = a_inter * jnp.dot(qc, s_old.astype(BF16), preferred_element_type=F32) \
            + jnp.dot(p.astype(BF16), vx.astype(BF16), preferred_element_type=F32)
        den = num[:, ML_DV:ML_DV + 1]
        h_ref[0] = num[:, :ML_DV] / jnp.maximum(jnp.abs(den), jnp.exp(-m_t))

    vw = (w_col * vx).astype(BF16)
    kt = kc.astype(F32).T.astype(BF16)
    s_scr[...] = a_state * s_old + jnp.dot(kt, vw, preferred_element_type=F32)
    m_scr[...] = jnp.broadcast_to(m_new, m_scr.shape)

    @pl.when(c == nc - 1)
    def _():
        so_ref[0, 0] = s_scr[...]
        mo_ref[0, 0] = m_scr[...]


def mlstm_scan(qk_act, proj, gcol, grow, s0, m0, rev, emit):
    bsz, t, _ = qk_act.shape
    nc = t // ML_CHUNK
    sw = ML_DV + ML_NX
    cm = (lambda c: nc - 1 - c) if rev else (lambda c: c)
    kq_blocks = ML_HEADS
    v_blocks = (2 * ML_HEADS * ML_DQK) // ML_DV
    in_specs = [
        pl.BlockSpec((1, ML_CHUNK, ML_DQK), lambda b, h, c: (b, cm(c), kq_blocks + h)),
        pl.BlockSpec((1, ML_CHUNK, ML_DV), lambda b, h, c: (b, cm(c), v_blocks + h)),
        pl.BlockSpec((1, 1, ML_CHUNK, 4), lambda b, h, c: (b, h, cm(c), 0)),
        pl.BlockSpec((1, 1, 4, ML_CHUNK), lambda b, h, c: (b, h, 0, cm(c))),
        pl.BlockSpec((1, 1, ML_DQK, sw), lambda b, h, c: (b, h, 0, 0)),
        pl.BlockSpec((1, 1, 8, 128), lambda b, h, c: (b, h, 0, 0)),
    ]
    args = [qk_act, proj, gcol, grow, s0, m0]
    out_specs = [pl.BlockSpec((1, 1, ML_DQK, sw), lambda b, h, c: (b, h, 0, 0)),
                 pl.BlockSpec((1, 1, 8, 128), lambda b, h, c: (b, h, 0, 0))]
    out_shape = [jax.ShapeDtypeStruct((bsz, ML_HEADS, ML_DQK, sw), F32),
                 jax.ShapeDtypeStruct((bsz, ML_HEADS, 8, 128), F32)]
    if emit:
        in_specs = [pl.BlockSpec((1, ML_CHUNK, ML_DQK), lambda b, h, c: (b, cm(c), h))] + in_specs
        args = [qk_act] + args
        out_specs = [pl.BlockSpec((1, ML_CHUNK, ML_DV), lambda b, h, c: (b, cm(c), h))] + out_specs
        out_shape = [jax.ShapeDtypeStruct((bsz, t, ML_HEADS * ML_DV), F32)] + out_shape
    outs = pl.pallas_call(
        functools.partial(_mlstm_kernel, rev=rev, emit=emit, nc=nc),
        grid=(bsz, ML_HEADS, nc),
        in_specs=in_specs,
        out_specs=out_specs,
        out_shape=out_shape,
        scratch_shapes=[pltpu.VMEM((ML_DQK, sw), F32), pltpu.VMEM((8, 128), F32)],
        compiler_params=_cparams(("parallel", "parallel", "arbitrary")),
        name="mlstm_" + ("bwd" if rev else "fwd") + ("_emit" if emit else "_state"),
    )(*args)
    if emit:
        return outs[0], outs[1], outs[2]
    return None, outs[0], outs[1]


def _ml_out_kernel(hf_ref, hb_ref, o_ref, hg_ref, w_ref, res_ref, gt_ref, out_ref, y_ref):
    @pl.when(pl.program_id(1) == 0)
    def _():
        h = hf_ref[...] + hb_ref[...]
        parts = []
        for hd in range(ML_HEADS):
            seg = h[:, hd * ML_DV:(hd + 1) * ML_DV]
            ms = jnp.mean(seg * seg, axis=-1, keepdims=True)
            parts.append(seg * lax.rsqrt(ms + EPS))
        hn = jnp.concatenate(parts, axis=1) * hg_ref[...]
        y_ref[...] = (hn * _sigmoid(o_ref[...].astype(F32))).astype(BF16)

    y = jnp.dot(y_ref[...], w_ref[...], preferred_element_type=F32)
    out_ref[...] = res_ref[...] + gt_ref[0] * y


def mlstm_out(h_f, h_b, proj, head_g, w_out, res, gt, rows_per_mod, tm=512, tn=1024):
    m, d = res.shape
    hw = h_f.shape[1]
    o_block = proj.shape[1] // hw - 1
    nb = gt.shape[0]
    mod_map = lambda i, j: ((i * tm) // rows_per_mod, 0, j)
    return pl.pallas_call(
        _ml_out_kernel,
        grid=(m // tm, d // tn),
        in_specs=[pl.BlockSpec((tm, hw), lambda i, j: (i, 0)),
                  pl.BlockSpec((tm, hw), lambda i, j: (i, 0)),
                  pl.BlockSpec((tm, hw), lambda i, j: (i, o_block)),
                  pl.BlockSpec((1, hw), lambda i, j: (0, 0)),
                  pl.BlockSpec((hw, tn), lambda i, j: (0, j)),
                  pl.BlockSpec((tm, tn), lambda i, j: (i, j)),
                  pl.BlockSpec((1, 1, tn), mod_map)],
        out_specs=pl.BlockSpec((tm, tn), lambda i, j: (i, j)),
        out_shape=jax.ShapeDtypeStruct((m, d), F32),
        scratch_shapes=[pltpu.VMEM((tm, hw), BF16)],
        compiler_params=_cparams(("parallel", "arbitrary")),
        name="mlstm_out",
    )(h_f, h_b, proj, head_g.reshape(1, hw), w_out, res, gt.reshape(nb, 1, d))


def _cm_out_kernel(z_ref, gv_ref, ws_ref, bs_ref, w_ref, res_ref, gt_ref, out_ref, y_ref, *, tm):
    width = gv_ref.shape[-1]
    gdim = width // CM_GROUPS

    @pl.when(pl.program_id(1) == 0)
    def _():
        for ch in range(tm // CM_CHUNK):
            rows = slice(ch * CM_CHUNK, (ch + 1) * CM_CHUNK)
            v = z_ref[rows, width:].astype(F32)
            v = v * lax.rsqrt(jnp.mean(v * v, axis=-1, keepdims=True) + EPS) * gv_ref[...]
            for g in range(CM_GROUPS):
                cols = slice(g * gdim, (g + 1) * gdim)
                sv = jnp.dot(ws_ref[g], v[:, cols].astype(BF16), preferred_element_type=F32) \
                    + bs_ref[:, g:g + 1]
                y_ref[rows, cols] = (z_ref[rows, cols].astype(F32) * sv).astype(BF16)

    y = jnp.dot(y_ref[...], w_ref[...], preferred_element_type=F32)
    out_ref[...] = res_ref[...] + gt_ref[0] * y


def chunk_mlp_out(z, g_v, w_s, b_s, w_out, res, gt, rows_per_mod, tm=512, tn=1024):
    m, d = res.shape
    width = g_v.shape[0]
    nb = gt.shape[0]
    mod_map = lambda i, j: ((i * tm) // rows_per_mod, 0, j)
    return pl.pallas_call(
        functools.partial(_cm_out_kernel, tm=tm),
        grid=(m // tm, d // tn),
        in_specs=[pl.BlockSpec((tm, 2 * width), lambda i, j: (i, 0)),
                  pl.BlockSpec((1, width), lambda i, j: (0, 0)),
                  pl.BlockSpec((CM_GROUPS, CM_CHUNK, CM_CHUNK), lambda i, j: (0, 0, 0)),
                  pl.BlockSpec((CM_CHUNK, CM_GROUPS), lambda i, j: (0, 0)),
                  pl.BlockSpec((width, tn), lambda i, j: (0, j)),
                  pl.BlockSpec((tm, tn), lambda i, j: (i, j)),
                  pl.BlockSpec((1, 1, tn), mod_map)],
        out_specs=pl.BlockSpec((tm, tn), lambda i, j: (i, j)),
        out_shape=jax.ShapeDtypeStruct((m, d), F32),
        scratch_shapes=[pltpu.VMEM((tm, width), BF16)],
        compiler_params=_cparams(("parallel", "arbitrary")),
        name="chunk_mlp_out",
    )(z, g_v.reshape(1, width), w_s, b_s.T, w_out, res, gt.reshape(nb, 1, d))


def _extract_top(s, payload, n_out):
    nrows = s.shape[0]
    rid = lax.broadcasted_iota(jnp.int32, s.shape, 0)
    vals, pays = [], []
    for _ in range(n_out):
        mx = jnp.max(s, axis=0, keepdims=True)
        pos = jnp.min(jnp.where(s == mx, rid, nrows), axis=0, keepdims=True)
        hit = rid == pos
        vals.append(mx)
        if payload is None:
            pays.append(pos)
        else:
            pays.append(jnp.sum(jnp.where(hit, payload, 0), axis=0, keepdims=True))
        s = jnp.where(hit, -jnp.inf, s)
    return jnp.concatenate(vals, axis=0), jnp.concatenate(pays, axis=0)


def _pair_candidates(s1, i1, s2, i2):
    k = PEER_TOPK
    sub = lax.broadcasted_iota(jnp.int32, (8, s1.shape[1]), 0)
    cand, cidx = [], []
    for bt in range(k // 8):
        cand.append(s1[0:1, :] + s2[bt * 8:(bt + 1) * 8, :])
        cidx.append(i1[0:1, :] * PEER_NKEYS + i2[bt * 8:(bt + 1) * 8, :])
    for a in range(1, 8):
        bmax = k // (a + 1) - 1
        c = s1[a:a + 1, :] + s2[0:8, :]
        cand.append(c if bmax >= 7 else jnp.where(sub <= bmax, c, -jnp.inf))
        cidx.append(i1[a:a + 1, :] * PEER_NKEYS + i2[0:8, :])
    cand.append(s1[8:k, :] + s2[0:1, :])
    cidx.append(i1[8:k, :] * PEER_NKEYS + i2[0:1, :])
    return jnp.concatenate(cand, axis=0), jnp.concatenate(cidx, axis=0)


def _peer_topk_kernel(h_ref, g_ref, sh_ref, sc_ref, wq_ref, keys_ref, hn_ref, idx_ref, gate_ref):
    hn = _adaln(h_ref[...], g_ref[...], sh_ref[0], sc_ref[0])
    hn_ref[...] = hn
    qt = lax.dot_general(wq_ref[...], hn.astype(BF16), (((1,), (1,)), ((), ())),
                         preferred_element_type=F32)
    idx_rows, gate_rows = [], []
    for hd in range(PEER_HEADS):
        top = []
        for p in range(2):
            hp = hd * 2 + p
            s = jnp.dot(keys_ref[hp], qt[hp * PEER_HALF:(hp + 1) * PEER_HALF, :],
                        preferred_element_type=F32, precision=HI)
            top.append(_extract_top(s, None, PEER_TOPK))
        (s1, i1), (s2, i2) = top
        cand, cidx = _pair_candidates(s1, i1, s2, i2)
        top_s, eidx = _extract_top(cand, cidx, PEER_TOPK)
        e = jnp.exp(top_s - top_s[0:1, :])
        gate_rows.append(e / jnp.sum(e, axis=0, keepdims=True))
        idx_rows.append(eidx)
    idx_ref[...] = jnp.concatenate(idx_rows, axis=0).T
    gate_ref[...] = jnp.concatenate(gate_rows, axis=0).T


def peer_topk(h, g, sh, sc, wq_t, keys, rows_per_mod, tm=256):
    m, d = h.shape
    tm = min(tm, m)
    nb = sh.shape[0]
    mod_map = lambda i: ((i * tm) // rows_per_mod, 0, 0)
    nq = wq_t.shape[0]
    return pl.pallas_call(
        _peer_topk_kernel,
        grid=(m // tm,),
        in_specs=[pl.BlockSpec((tm, d), lambda i: (i, 0)),
                  pl.BlockSpec((1, d), lambda i: (0, 0)),
                  pl.BlockSpec((1, 1, d), mod_map),
                  pl.BlockSpec((1, 1, d), mod_map),
                  pl.BlockSpec((nq, d), lambda i: (0, 0)),
                  pl.BlockSpec(keys.shape, lambda i: (0, 0, 0))],
        out_specs=[pl.BlockSpec((tm, d), lambda i: (i, 0)),
                   pl.BlockSpec((tm, PEER_SEL), lambda i: (i, 0)),
                   pl.BlockSpec((tm, PEER_SEL), lambda i: (i, 0))],
        out_shape=[jax.ShapeDtypeStruct((m, d), F32),
                   jax.ShapeDtypeStruct((m, PEER_SEL), jnp.int32),
                   jax.ShapeDtypeStruct((m, PEER_SEL), F32)],
        compiler_params=_cparams(("parallel",)),
        name="peer_topk",
    )(h, g.reshape(1, d), sh.reshape(nb, 1, d), sc.reshape(nb, 1, d), wq_t, keys)


PEER_SLOTS = 8
PEER_AHEAD = PEER_SLOTS - 1
PEER_DMA_QUEUES = 2


def _unpack_pair(w):
    lo = lax.bitcast_convert_type(lax.shift_left(w, jnp.int32(16)), F32)
    hi = lax.bitcast_convert_type(lax.bitwise_and(w, jnp.int32(-65536)), F32)
    return lo, hi


def _pack_kernel(u_ref, v_ref, o_ref):
    te, d = u_ref.shape[1:]
    nch = d // 128
    half = nch // 2

    def words(ref, rows, r):
        lo = lax.bitcast_convert_type(ref[0, rows, r * 128:(r + 1) * 128].astype(BF16).astype(F32), jnp.int32)
        hi = lax.bitcast_convert_type(
            ref[0, rows, (r + half) * 128:(r + half + 1) * 128].astype(BF16).astype(F32), jnp.int32)
        return lax.bitwise_or(lax.bitwise_and(hi, jnp.int32(-65536)), lax.shift_right_logical(lo, jnp.int32(16)))

    for g in range(te // 8):
        rows = slice(g * 8, (g + 1) * 8)
        for r in range(half):
            o_ref[pl.ds(g * 8 * nch + r, 8, stride=nch), :] = words(u_ref, rows, r)
            o_ref[pl.ds(g * 8 * nch + half + r, 8, stride=nch), :] = words(v_ref, rows, r)


def pack_expert_table(u_all, v_all, layer, te=128):
    _, ne, d = u_all.shape
    nch = d // 128
    out = pl.pallas_call(
        _pack_kernel,
        grid=(ne // te,),
        in_specs=[pl.BlockSpec((1, te, d), lambda i: (layer, i, 0)),
                  pl.BlockSpec((1, te, d), lambda i: (layer, i, 0))],
        out_specs=pl.BlockSpec((te * nch, 128), lambda i: (i, 0)),
        out_shape=jax.ShapeDtypeStruct((ne * nch, 128), jnp.int32),
        compiler_params=_cparams(("parallel",)),
        name="pack_expert_table",
    )(u_all, v_all)
    return out.reshape(ne, nch, 128)


def _peer_gather_kernel(idx_hbm, gate_ref, x_ref, res_ref, gt_ref, tab_hbm, out_ref,
                        idx_smem, buf, sems, idx_sem, act_ref, zred_ref, xt_ref, yt_ref, *, tb):
    d = x_ref.shape[1]
    nrow = d // 128
    nvr = nrow // 8
    npk = nvr // 2
    blk = pl.program_id(0)

    idx_cp = pltpu.make_async_copy(idx_hbm.at[pl.ds(blk * (tb * PEER_SEL), tb * PEER_SEL)], idx_smem, idx_sem)
    idx_cp.start()
    idx_cp.wait()

    def start_row(base, slot, k):
        pltpu.make_async_copy(tab_hbm.at[idx_smem[base + k]], buf.at[slot, k], sems.at[slot]).start(
            priority=k % PEER_DMA_QUEUES)

    def issue(t, slot):
        for k in range(PEER_SEL):
            start_row(t * PEER_SEL, slot, k)

    def issuer(t, slot):
        nxt = [0]

        def pump(n):
            for _ in range(n):
                if nxt[0] < PEER_SEL:
                    start_row(t * PEER_SEL, slot, nxt[0])
                    nxt[0] += 1
        return pump

    def wait(slot):
        pltpu.make_async_copy(tab_hbm.at[pl.ds(0, PEER_SEL)], buf.at[slot], sems.at[slot]).wait()

    last = tb - 1
    no_pump = lambda n: None

    for t0 in range(PEER_AHEAD):
        issue(t0, t0)

    def to_tiles(g, carry):
        r0 = pl.multiple_of(g * 8, 8)
        for c in range(nrow):
            xt_ref[pl.ds(g * (8 * nrow) + c, 8, stride=nrow), :] = x_ref[pl.ds(r0, 8), c * 128:(c + 1) * 128]
        return carry

    lax.fori_loop(0, tb // 8, to_tiles, 0)

    def reduce_phase(t, slot, zslot, pump):
        xs = [xt_ref[pl.ds(pl.multiple_of(t * nrow + r * 8, 8), 8), :] for r in range(nvr)]
        lane = lax.broadcasted_iota(jnp.int32, (8, 128), 1)
        nacc = 4
        accs = [jnp.zeros((8, 128), F32) for _ in range(nacc)]
        for k in range(PEER_SEL):
            p = None
            for j in range(npk):
                lo, hi = _unpack_pair(buf[slot, k, j * 8:(j + 1) * 8, :])
                q = lo * xs[j] + hi * xs[j + npk]
                p = q if p is None else p + q
            row_sums = jnp.sum(p, axis=1, keepdims=True)
            accs[k % nacc] = jnp.where(lane == k, row_sums, accs[k % nacc])
            if k % 8 in (1, 4, 7):
                pump(1)
        zred_ref[zslot] = (accs[0] + accs[1]) + (accs[2] + accs[3])

    def act_values(t, zslot):
        ones = jnp.ones((128, 128), F32)
        diag = (lax.broadcasted_iota(jnp.int32, (128, 128), 0)
                == lax.broadcasted_iota(jnp.int32, (128, 128), 1))
        score = jnp.sum(zred_ref[zslot], axis=0, keepdims=True)
        gates = gate_ref[pl.ds(pl.multiple_of((t // 8) * 8, 8), 8), :]
        sub = lax.broadcasted_iota(jnp.int32, (8, 128), 0)
        gate = jnp.sum(jnp.where(sub == t % 8, gates, 0.0), axis=0, keepdims=True)
        act = _gelu(score) * gate
        return jnp.dot(jnp.where(diag, act, 0.0), ones, preferred_element_type=F32, precision=HI)

    def mix_phase(t, slot, aslot, pump):
        nacc = 4
        accs = [[None] * nvr for _ in range(nacc)]
        for k in range(PEER_SEL):
            a = jnp.broadcast_to(act_ref[aslot, k:k + 1, :], (8, 128))
            for j in range(npk):
                pair = _unpack_pair(buf[slot, k, (npk + j) * 8:(npk + j + 1) * 8, :])
                for r, val in ((j, pair[0]), (j + npk, pair[1])):
                    term = a * val
                    prev = accs[k % nacc][r]
                    accs[k % nacc][r] = term if prev is None else prev + term
            if k % 2 == 0 and k < 112:
                pump(1)
        for r in range(nvr):
            tot = accs[0][r]
            for j in range(1, nacc):
                tot = tot + accs[j][r]
            yt_ref[pl.ds(pl.multiple_of(t * nrow + r * 8, 8), 8), :] = tot

    assert PEER_AHEAD >= 3
    wait(0)
    reduce_phase(0, 0, 0, no_pump)
    act_ref[0] = act_values(0, 0)
    wait(1)
    reduce_phase(1, 1, 1, no_pump)

    def trip(t, carry):
        ta = t + PEER_AHEAD
        pump = issuer(jnp.minimum(ta, last), ta % PEER_SLOTS)
        t1 = t + 1
        act_next = act_values(jnp.minimum(t1, last), t1 % 2)
        mix_phase(t, t % PEER_SLOTS, t % 2, pump)
        t2 = t + 2
        wait(t2 % PEER_SLOTS)
        reduce_phase(jnp.minimum(t2, last), t2 % PEER_SLOTS, t2 % 2, pump)
        pump(PEER_SEL)
        act_ref[t1 % 2] = act_next
        return carry

    lax.fori_loop(0, tb, trip, 0)
    for e in range(2, PEER_AHEAD):
        wait((tb + e) % PEER_SLOTS)

    def from_tiles(g, carry):
        r0 = pl.multiple_of(g * 8, 8)
        for c in range(nrow):
            cols = slice(c * 128, (c + 1) * 128)
            y = yt_ref[pl.ds(g * (8 * nrow) + c, 8, stride=nrow), :]
            out_ref[pl.ds(r0, 8), cols] = res_ref[pl.ds(r0, 8), cols] + gt_ref[0, :, cols] * y
        return carry

    lax.fori_loop(0, tb // 8, from_tiles, 0)


def peer_gather(idx, gate, x, res, gt, table, rows_per_mod, tb=512):
    m, d = x.shape
    tb = min(tb, m)
    nb = gt.shape[0]
    nrow = d // 128
    mod_map = lambda i: ((i * tb) // rows_per_mod, 0, 0)
    tok_spec = pl.BlockSpec((tb, d), lambda i: (i, 0))
    return pl.pallas_call(
        functools.partial(_peer_gather_kernel, tb=tb),
        grid=(m // tb,),
        in_specs=[pl.BlockSpec(memory_space=pl.ANY),
                  pl.BlockSpec((tb, PEER_SEL), lambda i: (i, 0)),
                  tok_spec,
                  tok_spec,
                  pl.BlockSpec((1, 1, d), mod_map),
                  pl.BlockSpec(memory_space=pl.ANY)],
        out_specs=tok_spec,
        out_shape=jax.ShapeDtypeStruct((m, d), F32),
        scratch_shapes=[pltpu.SMEM((tb * PEER_SEL,), jnp.int32),
                        pltpu.VMEM((PEER_SLOTS, PEER_SEL, nrow, 128), jnp.int32),
                        pltpu.SemaphoreType.DMA((PEER_SLOTS,)),
                        pltpu.SemaphoreType.DMA(()),
                        pltpu.VMEM((2, PEER_SEL, 128), F32),
                        pltpu.VMEM((2, 8, 128), F32),
                        pltpu.VMEM((tb * nrow, 128), F32),
                        pltpu.VMEM((tb * nrow, 128), F32)],
        compiler_params=_cparams(("arbitrary",)),
        name="peer_gather",
    )(idx.reshape(-1), gate, x, res, gt.reshape(nb, 1, d), table)


def _rmsnorm_kernel(x_ref, g_ref, o_ref):
    x = x_ref[...]
    o_ref[...] = x * lax.rsqrt(jnp.mean(x * x, axis=-1, keepdims=True) + EPS) * g_ref[...]


def rmsnorm(x, g, tm=512):
    m, d = x.shape
    return pl.pallas_call(
        _rmsnorm_kernel,
        grid=(m // tm,),
        in_specs=[pl.BlockSpec((tm, d), lambda i: (i, 0)), pl.BlockSpec((1, d), lambda i: (0, 0))],
        out_specs=pl.BlockSpec((tm, d), lambda i: (i, 0)),
        out_shape=jax.ShapeDtypeStruct((m, d), F32),
        compiler_params=_cparams(("parallel",)),
        name="final_rmsnorm",
    )(x, g.reshape(1, d))


def _split_gates(gcol, grow, bsz, t):
    gc = gcol.reshape(bsz, t, 4, ML_HEADS).transpose(0, 3, 1, 2)
    gr = grow.reshape(4, ML_HEADS, bsz, t).transpose(2, 1, 0, 3)
    return gc, gr


def _mlstm_layer(h_lat, h_ctx, mods, cmods, g_mix, w_in, conv_w, conv_b, gate_b, head_g, w_out):
    bsz, t, d = h_lat.shape
    tc = h_ctx.shape[1]
    sh1, sc1, gt1 = mods
    csh1, csc1 = cmods
    qk_w = 2 * ML_HEADS * ML_DQK
    n_main = qk_w + 2 * ML_HEADS * ML_DV
    w_main = w_in[:, :n_main].astype(BF16)
    w_g = w_in[:, n_main:]
    k_scale = jnp.concatenate([jnp.ones((qk_w // 2,), F32), jnp.full((qk_w // 2,), ML_DQK ** -0.5, F32)])
    sw = ML_DV + ML_NX

    xc = h_ctx.reshape(bsz * tc, d)
    projc = ln_matmul(xc, g_mix, csh1, csc1, w_main, bsz * tc).reshape(bsz, tc, n_main)
    gcc, grc = gate_preacts(xc, g_mix, csh1, csc1, w_g, gate_b, bsz * tc)
    gcc, grc = _split_gates(gcc, grc, bsz, tc)
    qkc = conv_silu(projc, conv_w, conv_b, k_scale, qk_w, on_grid=False)
    s0 = jnp.zeros((bsz, ML_HEADS, ML_DQK, sw), F32)
    m0 = jnp.zeros((bsz, ML_HEADS, 8, 128), F32)
    _, sf, mf = mlstm_scan(qkc, projc, gcc, grc, s0, m0, rev=False, emit=False)
    _, sb, mb = mlstm_scan(qkc, projc, gcc, grc, s0, m0, rev=True, emit=False)

    x2 = h_lat.reshape(bsz * t, d)
    proj = ln_matmul(x2, g_mix, sh1, sc1, w_main, t).reshape(bsz, t, n_main)
    gcl, grl = gate_preacts(x2, g_mix, sh1, sc1, w_g, gate_b, t)
    gcl, grl = _split_gates(gcl, grl, bsz, t)
    qk = conv_silu(proj, conv_w, conv_b, k_scale, qk_w, on_grid=True)
    h_f, _, _ = mlstm_scan(qk, proj, gcl, grl, sf, mf, rev=False, emit=True)
    h_b, _, _ = mlstm_scan(qk, proj, gcl, grl, sb, mb, rev=True, emit=True)
    hw = ML_HEADS * ML_DV
    out = mlstm_out(h_f.reshape(bsz * t, hw), h_b.reshape(bsz * t, hw), proj.reshape(bsz * t, n_main),
                    head_g.reshape(-1), w_out.astype(BF16), x2, gt1, t)
    return out


def _chunk_mlp_layer(x2, t, mods, g_mix, w_in, g_v, w_s, b_s, w_out):
    sh1, sc1, gt1 = mods
    z = ln_matmul(x2, g_mix, sh1, sc1, w_in.astype(BF16), t, act="gelu")
    return chunk_mlp_out(z, g_v, w_s.astype(BF16), b_s, w_out.astype(BF16), x2, gt1, t)


def _peer_layer(x2, t, mods, g_chan, w_q, keys, u_all, v_all, layer):
    sh2, sc2, gt2 = mods
    hn, idx, gate = peer_topk(x2, g_chan, sh2, sc2, w_q.T.astype(BF16),
                              keys.reshape(PEER_HEADS * 2, PEER_NKEYS, PEER_HALF), t)
    return peer_gather(idx, gate, hn, x2, gt2, pack_expert_table(u_all, v_all, layer), t)


def kernel(x, c, ctx, c_ctx, w_mod, b_mod, g_mix, g_chan, ml_w_in, ml_conv_w, ml_conv_b, ml_gate_b, ml_head_g, ml_w_out, cm_w_in, cm_g_v, cm_w_s, cm_b_s, cm_w_out, peer_w_q, peer_keys, peer_u, peer_v, g_final):
    bsz, t, d = x.shape
    depth = w_mod.shape[0]
    cond = jnp.zeros((8, d), F32).at[:bsz].set(c).at[bsz].set(c_ctx)
    mods = modulation(cond, w_mod, b_mod)
    h2 = x.reshape(bsz * t, d)
    for i in range(depth):
        j = i // 2
        mi = mods[i].reshape(8, 6, d)
        lat = [mi[:bsz, s] for s in range(6)]
        cx = [mi[bsz:bsz + 1, s] for s in range(6)]
        if i % 2 == 0:
            h2 = _mlstm_layer(h2.reshape(bsz, t, d), ctx, lat[:3], cx[:2], g_mix[i], ml_w_in[j], ml_conv_w[j],
                              ml_conv_b[j], ml_gate_b[j], ml_head_g[j], ml_w_out[j])
        else:
            h2 = _chunk_mlp_layer(h2, t, lat[:3], g_mix[i], cm_w_in[j], cm_g_v[j], cm_w_s[j], cm_b_s[j],
                                  cm_w_out[j])
        h2 = _peer_layer(h2, t, lat[3:], g_chan[i], peer_w_q[i], peer_keys[i], peer_u, peer_v, i)
    return rmsnorm(h2, g_final).reshape(bsz, t, d)
```

```python
import functools

import jax
import jax.numpy as jnp
from jax import lax
from jax.experimental import pallas as pl
from jax.experimental.pallas import tpu as pltpu

F32 = jnp.float32
BF16 = jnp.bfloat16
EPS = 1e-6

GRID_W = 64
ML_HEADS = 4
ML_DQK = 256
ML_DV = 512
ML_CHUNK = 128
ML_NX = 128
CM_CHUNK = 128
CM_GROUPS = 8
PEER_HEADS = 8
PEER_NKEYS = 128
PEER_HALF = 128
PEER_TOPK = 16
PEER_SEL = PEER_HEADS * PEER_TOPK

VMEM_LIMIT = 48 * 1024 * 1024
HI = lax.Precision.HIGHEST


def _cparams(sem):
    return pltpu.CompilerParams(dimension_semantics=sem, vmem_limit_bytes=VMEM_LIMIT)


def _sigmoid(x):
    return 1.0 / (1.0 + jnp.exp(-x))


def _gelu(x):
    return 0.5 * x * (1.0 + jnp.tanh(0.7978845608028654 * (x + 0.044715 * (x * x * x))))


def _log_sigmoid(x):
    return jnp.minimum(x, 0.0) - jnp.log(1.0 + jnp.exp(-jnp.abs(x)))


def _adaln(x, g, sh, sc):
    ms = jnp.mean(x * x, axis=-1, keepdims=True)
    return (x * lax.rsqrt(ms + EPS) * g) * (1.0 + sc) + sh


def _mod_kernel(c_ref, w_ref, b_ref, o_ref):
    a = c_ref[...]
    a = a * _sigmoid(a)
    o_ref[0] = jnp.dot(a, w_ref[0], preferred_element_type=F32) + b_ref[0]


def modulation(cond, w_mod, b_mod, tn=1024):
    nl, d, n = w_mod.shape
    return pl.pallas_call(
        _mod_kernel,
        grid=(nl, n // tn),
        in_specs=[pl.BlockSpec((8, d), lambda l, j: (0, 0)),
                  pl.BlockSpec((1, d, tn), lambda l, j: (l, 0, j)),
                  pl.BlockSpec((1, 1, tn), lambda l, j: (l, 0, j))],
        out_specs=pl.BlockSpec((1, 8, tn), lambda l, j: (l, 0, j)),
        out_shape=jax.ShapeDtypeStruct((nl, 8, n), F32),
        compiler_params=_cparams(("parallel", "parallel")),
        name="modulation",
    )(cond, w_mod, b_mod.reshape(nl, 1, n))


def _ln_mm_kernel(x_ref, g_ref, sh_ref, sc_ref, w_ref, o_ref, hn_ref, *, act):
    @pl.when(pl.program_id(1) == 0)
    def _():
        hn_ref[...] = _adaln(x_ref[...], g_ref[...], sh_ref[0], sc_ref[0]).astype(BF16)

    r = jnp.dot(hn_ref[...], w_ref[...], preferred_element_type=F32)
    if act == "gelu":
        r = _gelu(r)
    o_ref[...] = r.astype(o_ref.dtype)


def ln_matmul(x, g, sh, sc, w, rows_per_mod, act=None, tm=512, tn=1024):
    m, d = x.shape
    n = w.shape[1]
    tm = min(tm, m)
    nb = sh.shape[0]
    mod_map = lambda i, j: ((i * tm) // rows_per_mod, 0, 0)
    return pl.pallas_call(
        functools.partial(_ln_mm_kernel, act=act),
        grid=(m // tm, n // tn),
        in_specs=[pl.BlockSpec((tm, d), lambda i, j: (i, 0)),
                  pl.BlockSpec((1, d), lambda i, j: (0, 0)),
                  pl.BlockSpec((1, 1, d), mod_map),
                  pl.BlockSpec((1, 1, d), mod_map),
                  pl.BlockSpec((d, tn), lambda i, j: (0, j))],
        out_specs=pl.BlockSpec((tm, tn), lambda i, j: (i, j)),
        out_shape=jax.ShapeDtypeStruct((m, n), BF16),
        scratch_shapes=[pltpu.VMEM((tm, d), BF16)],
        compiler_params=_cparams(("parallel", "arbitrary")),
        name="ln_matmul",
    )(x, g.reshape(1, d), sh.reshape(nb, 1, d), sc.reshape(nb, 1, d), w)


def _gates_kernel(x_ref, g_ref, sh_ref, sc_ref, w_ref, wt_ref, b_ref, bt_ref, oc_ref, or_ref):
    hn = _adaln(x_ref[...], g_ref[...], sh_ref[0], sc_ref[0])
    oc_ref[...] = jnp.dot(hn, w_ref[...], preferred_element_type=F32) + b_ref[...]
    or_ref[...] = lax.dot_general(wt_ref[...], hn, (((1,), (1,)), ((), ())),
                                  preferred_element_type=F32) + bt_ref[...]


def gate_preacts(x, g, sh, sc, w_g, b_g, rows_per_mod, tm=512):
    m, d = x.shape
    ng = w_g.shape[1]
    tm = min(tm, m)
    nb = sh.shape[0]
    mod_map = lambda i: ((i * tm) // rows_per_mod, 0, 0)
    return pl.pallas_call(
        _gates_kernel,
        grid=(m // tm,),
        in_specs=[pl.BlockSpec((tm, d), lambda i: (i, 0)),
                  pl.BlockSpec((1, d), lambda i: (0, 0)),
                  pl.BlockSpec((1, 1, d), mod_map),
                  pl.BlockSpec((1, 1, d), mod_map),
                  pl.BlockSpec((d, ng), lambda i: (0, 0)),
                  pl.BlockSpec((ng, d), lambda i: (0, 0)),
                  pl.BlockSpec((1, ng), lambda i: (0, 0)),
                  pl.BlockSpec((ng, 1), lambda i: (0, 0))],
        out_specs=[pl.BlockSpec((tm, ng), lambda i: (i, 0)),
                   pl.BlockSpec((ng, tm), lambda i: (0, i))],
        out_shape=[jax.ShapeDtypeStruct((m, ng), F32),
                   jax.ShapeDtypeStruct((ng, m), F32)],
        compiler_params=_cparams(("parallel",)),
        name="gate_preacts",
    )(x, g.reshape(1, d), sh.reshape(nb, 1, d), sc.reshape(nb, 1, d),
      w_g, w_g.T, b_g.reshape(1, ng), b_g.reshape(ng, 1))


def _conv_kernel(*refs, on_grid, tt, nt):
    if on_grid:
        xp_ref, xc_ref, xn_ref, w_ref, b_ref, s_ref, o_ref = refs
    else:
        xc_ref, w_ref, b_ref, s_ref, o_ref = refs
    i = pl.program_id(1)
    ct = xc_ref.shape[-1]
    cur = xc_ref[0].astype(F32)
    tok = lax.broadcasted_iota(jnp.int32, (tt, ct), 0)
    if on_grid:
        first = (tok % GRID_W) == 0
        last = (tok % GRID_W) == GRID_W - 1
    else:
        first = tok == 0
        last = tok == tt - 1

    def row_taps(base, dr):
        left = jnp.where(first, 0.0, pltpu.roll(base, 1, axis=0))
        right = jnp.where(last, 0.0, pltpu.roll(base, tt - 1, axis=0))
        return left * w_ref[dr, 0:1, :] + base * w_ref[dr, 1:2, :] + right * w_ref[dr, 2:3, :]

    y = row_taps(cur, 1)
    if on_grid:
        prev = jnp.where(i > 0, xp_ref[0].astype(F32), 0.0)
        nxt = jnp.where(i < nt - 1, xn_ref[0].astype(F32), 0.0)
        up = jnp.concatenate([prev, cur[:tt - GRID_W]], axis=0)
        down = jnp.concatenate([cur[GRID_W:], nxt], axis=0)
        y = y + row_taps(up, 0) + row_taps(down, 2)
    y = y + b_ref[...]
    y = y * _sigmoid(y)
    o_ref[0] = (y * s_ref[...]).astype(o_ref.dtype)


def conv_silu(proj, conv_w, conv_b, out_scale, n_ch, on_grid, tt=512, ct=512):
    bsz, t, _ = proj.shape
    if not on_grid:
        tt = t
    nt = t // tt
    rpt = tt // GRID_W
    kern = functools.partial(_conv_kernel, on_grid=on_grid, tt=tt, nt=nt)
    cur_spec = pl.BlockSpec((1, tt, ct), lambda b, i, j: (b, i, j))
    par_specs = [pl.BlockSpec((3, 3, ct), lambda b, i, j: (0, 0, j)),
                 pl.BlockSpec((1, ct), lambda b, i, j: (0, j)),
                 pl.BlockSpec((1, ct), lambda b, i, j: (0, j))]
    if on_grid:
        nrow = t // GRID_W
        in_specs = [pl.BlockSpec((1, GRID_W, ct), lambda b, i, j: (b, jnp.maximum(i * rpt - 1, 0), j)),
                    cur_spec,
                    pl.BlockSpec((1, GRID_W, ct), lambda b, i, j: (b, jnp.minimum((i + 1) * rpt, nrow - 1), j))]
        args = (proj, proj, proj)
    else:
        in_specs = [cur_spec]
        args = (proj,)
    return pl.pallas_call(
        kern,
        grid=(bsz, nt, n_ch // ct),
        in_specs=in_specs + par_specs,
        out_specs=pl.BlockSpec((1, tt, ct), lambda b, i, j: (b, i, j)),
        out_shape=jax.ShapeDtypeStruct((bsz, t, n_ch), BF16),
        compiler_params=_cparams(("parallel", "parallel", "parallel")),
        name="conv_silu_grid" if on_grid else "conv_silu_seq",
    )(*args, conv_w, conv_b.reshape(1, n_ch), out_scale.reshape(1, n_ch))


def _mlstm_kernel(*refs, rev, emit, nc):
    if emit:
        (q_ref, k_ref, v_ref, gc_ref, gr_ref, s0_ref, m0_ref,
         h_ref, so_ref, mo_ref, s_scr, m_scr) = refs
    else:
        (k_ref, v_ref, gc_ref, gr_ref, s0_ref, m0_ref,
         so_ref, mo_ref, s_scr, m_scr) = refs
    c = pl.program_id(2)
    ln = ML_CHUNK
    gi = 2 if rev else 0

    @pl.when(c == 0)
    def _():
        s_scr[...] = s0_ref[0, 0]
        m_scr[...] = m0_ref[0, 0]

    gc = gc_ref[0, 0]
    gr = gr_ref[0, 0]
    i_col = gc[:, gi:gi + 1]
    f_col = _log_sigmoid(gc[:, gi + 1:gi + 2])
    i_row = gr[gi:gi + 1, :]
    f_row = _log_sigmoid(gr[gi + 1:gi + 2, :])
    row = lax.broadcasted_iota(jnp.int32, (ln, ln), 0)
    col = lax.broadcasted_iota(jnp.int32, (ln, ln), 1)
    seen = (col >= row) if rev else (col <= row)
    upto = (row >= col) if rev else (row <= col)
    b_col = jnp.sum(jnp.where(seen, f_row, 0.0), axis=1, keepdims=True)
    b_row = jnp.sum(jnp.where(upto, f_col, 0.0), axis=0, keepdims=True)
    b_end = jnp.sum(f_row, axis=1, keepdims=True)
    m_old = m_scr[0:1, 0:1]
    w_log = b_end - b_row + i_row
    m_new = jnp.maximum(b_end + m_old, jnp.max(w_log, axis=1, keepdims=True))
    a_state = jnp.exp(b_end + m_old - m_new)
    w_col = jnp.exp(b_end - b_col + i_col - m_new)

    kc = k_ref[0]
    vc = v_ref[0]
    lane = lax.broadcasted_iota(jnp.int32, (ln, ML_NX), 1)
    ones_col = jnp.where(lane == 0, 1.0, 0.0)
    vx = jnp.concatenate([vc.astype(F32), ones_col], axis=1)
    s_old = s_scr[...]

    if emit:
        qc = q_ref[0]
        inter = b_col + m_old
        d_log = jnp.where(seen, b_col - b_row + i_row, -jnp.inf)
        m_t = jnp.maximum(inter, jnp.max(d_log, axis=1, keepdims=True))
        a_inter = jnp.exp(inter - m_t)
        qk = lax.dot_general(qc, kc, (((1,), (1,)), ((), ())), preferred_element_type=F32)
        p = jnp.exp(d_log - m_t) * qk
        num = a_inter * jnp.dot(qc, s_old.astype(BF16), preferred_element_type=F32) \
            + jnp.dot(p.astype(BF16), vx.astype(BF16), preferred_element_type=F32)
        den = num[:, ML_DV:ML_DV + 1]
        h_ref[0] = num[:, :ML_DV] / jnp.maximum(jnp.abs(den), jnp.exp(-m_t))

    vw = (w_col * vx).astype(BF16)
    kt = kc.astype(F32).T.astype(BF16)
    s_scr[...] = a_state * s_old + jnp.dot(kt, vw, preferred_element_type=F32)
    m_scr[...] = jnp.broadcast_to(m_new, m_scr.shape)

    @pl.when(c == nc - 1)
    def _():
        so_ref[0, 0] = s_scr[...]
        mo_ref[0, 0] = m_scr[...]


def mlstm_scan(qk_act, proj, gcol, grow, s0, m0, rev, emit):
    bsz, t, _ = qk_act.shape
    nc = t // ML_CHUNK
    sw = ML_DV + ML_NX
    cm = (lambda c: nc - 1 - c) if rev else (lambda c: c)
    kq_blocks = ML_HEADS
    v_blocks = (2 * ML_HEADS * ML_DQK) // ML_DV
    in_specs = [
        pl.BlockSpec((1, ML_CHUNK, ML_DQK), lambda b, h, c: (b, cm(c), kq_blocks + h)),
        pl.BlockSpec((1, ML_CHUNK, ML_DV), lambda b, h, c: (b, cm(c), v_blocks + h)),
        pl.BlockSpec((1, 1, ML_CHUNK, 4), lambda b, h, c: (b, h, cm(c), 0)),
        pl.BlockSpec((1, 1, 4, ML_CHUNK), lambda b, h, c: (b, h, 0, cm(c))),
        pl.BlockSpec((1, 1, ML_DQK, sw), lambda b, h, c: (b, h, 0, 0)),
        pl.BlockSpec((1, 1, 8, 128), lambda b, h, c: (b, h, 0, 0)),
    ]
    args = [qk_act, proj, gcol, grow, s0, m0]
    out_specs = [pl.BlockSpec((1, 1, ML_DQK, sw), lambda b, h, c: (b, h, 0, 0)),
                 pl.BlockSpec((1, 1, 8, 128), lambda b, h, c: (b, h, 0, 0))]
    out_shape = [jax.ShapeDtypeStruct((bsz, ML_HEADS, ML_DQK, sw), F32),
                 jax.ShapeDtypeStruct((bsz, ML_HEADS, 8, 128), F32)]
    if emit:
        in_specs = [pl.BlockSpec((1, ML_CHUNK, ML_DQK), lambda b, h, c: (b, cm(c), h))] + in_specs
        args = [qk_act] + args
        out_specs = [pl.BlockSpec((1, ML_CHUNK, ML_DV), lambda b, h, c: (b, cm(c), h))] + out_specs
        out_shape = [jax.ShapeDtypeStruct((bsz, t, ML_HEADS * ML_DV), F32)] + out_shape
    outs = pl.pallas_call(
        functools.partial(_mlstm_kernel, rev=rev, emit=emit, nc=nc),
        grid=(bsz, ML_HEADS, nc),
        in_specs=in_specs,
        out_specs=out_specs,
        out_shape=out_shape,
        scratch_shapes=[pltpu.VMEM((ML_DQK, sw), F32), pltpu.VMEM((8, 128), F32)],
        compiler_params=_cparams(("parallel", "parallel", "arbitrary")),
        name="mlstm_" + ("bwd" if rev else "fwd") + ("_emit" if emit else "_state"),
    )(*args)
    if emit:
        return outs[0], outs[1], outs[2]
    return None, outs[0], outs[1]


def _ml_out_kernel(hf_ref, hb_ref, o_ref, hg_ref, w_ref, res_ref, gt_ref, out_ref, y_ref):
    @pl.when(pl.program_id(1) == 0)
    def _():
        h = hf_ref[...] + hb_ref[...]
        parts = []
        for hd in range(ML_HEADS):
            seg = h[:, hd * ML_DV:(hd + 1) * ML_DV]
            ms = jnp.mean(seg * seg, axis=-1, keepdims=True)
            parts.append(seg * lax.rsqrt(ms + EPS))
        hn = jnp.concatenate(parts, axis=1) * hg_ref[...]
        y_ref[...] = (hn * _sigmoid(o_ref[...].astype(F32))).astype(BF16)

    y = jnp.dot(y_ref[...], w_ref[...], preferred_element_type=F32)
    out_ref[...] = res_ref[...] + gt_ref[0] * y


def mlstm_out(h_f, h_b, proj, head_g, w_out, res, gt, rows_per_mod, tm=512, tn=1024):
    m, d = res.shape
    hw = h_f.shape[1]
    o_block = proj.shape[1] // hw - 1
    nb = gt.shape[0]
    mod_map = lambda i, j: ((i * tm) // rows_per_mod, 0, j)
    return pl.pallas_call(
        _ml_out_kernel,
        grid=(m // tm, d // tn),
        in_specs=[pl.BlockSpec((tm, hw), lambda i, j: (i, 0)),
                  pl.BlockSpec((tm, hw), lambda i, j: (i, 0)),
                  pl.BlockSpec((tm, hw), lambda i, j: (i, o_block)),
                  pl.BlockSpec((1, hw), lambda i, j: (0, 0)),
                  pl.BlockSpec((hw, tn), lambda i, j: (0, j)),
                  pl.BlockSpec((tm, tn), lambda i, j: (i, j)),
                  pl.BlockSpec((1, 1, tn), mod_map)],
        out_specs=pl.BlockSpec((tm, tn), lambda i, j: (i, j)),
        out_shape=jax.ShapeDtypeStruct((m, d), F32),
        scratch_shapes=[pltpu.VMEM((tm, hw), BF16)],
        compiler_params=_cparams(("parallel", "arbitrary")),
        name="mlstm_out",
    )(h_f, h_b, proj, head_g.reshape(1, hw), w_out, res, gt.reshape(nb, 1, d))


def _cm_out_kernel(z_ref, gv_ref, ws_ref, bs_ref, w_ref, res_ref, gt_ref, out_ref, y_ref, *, tm):
    width = gv_ref.shape[-1]
    gdim = width // CM_GROUPS

    @pl.when(pl.program_id(1) == 0)
    def _():
        for ch in range(tm // CM_CHUNK):
            rows = slice(ch * CM_CHUNK, (ch + 1) * CM_CHUNK)
            v = z_ref[rows, width:].astype(F32)
            v = v * lax.rsqrt(jnp.mean(v * v, axis=-1, keepdims=True) + EPS) * gv_ref[...]
            for g in range(CM_GROUPS):
                cols = slice(g * gdim, (g + 1) * gdim)
                sv = jnp.dot(ws_ref[g], v[:, cols].astype(BF16), preferred_element_type=F32) \
                    + bs_ref[:, g:g + 1]
                y_ref[rows, cols] = (z_ref[rows, cols].astype(F32) * sv).astype(BF16)

    y = jnp.dot(y_ref[...], w_ref[...], preferred_element_type=F32)
    out_ref[...] = res_ref[...] + gt_ref[0] * y


def chunk_mlp_out(z, g_v, w_s, b_s, w_out, res, gt, rows_per_mod, tm=512, tn=1024):
    m, d = res.shape
    width = g_v.shape[0]
    nb = gt.shape[0]
    mod_map = lambda i, j: ((i * tm) // rows_per_mod, 0, j)
    return pl.pallas_call(
        functools.partial(_cm_out_kernel, tm=tm),
        grid=(m // tm, d // tn),
        in_specs=[pl.BlockSpec((tm, 2 * width), lambda i, j: (i, 0)),
                  pl.BlockSpec((1, width), lambda i, j: (0, 0)),
                  pl.BlockSpec((CM_GROUPS, CM_CHUNK, CM_CHUNK), lambda i, j: (0, 0, 0)),
                  pl.BlockSpec((CM_CHUNK, CM_GROUPS), lambda i, j: (0, 0)),
                  pl.BlockSpec((width, tn), lambda i, j: (0, j)),
                  pl.BlockSpec((tm, tn), lambda i, j: (i, j)),
                  pl.BlockSpec((1, 1, tn), mod_map)],
        out_specs=pl.BlockSpec((tm, tn), lambda i, j: (i, j)),
        out_shape=jax.ShapeDtypeStruct((m, d), F32),
        scratch_shapes=[pltpu.VMEM((tm, width), BF16)],
        compiler_params=_cparams(("parallel", "arbitrary")),
        name="chunk_mlp_out",
    )(z, g_v.reshape(1, width), w_s, b_s.T, w_out, res, gt.reshape(nb, 1, d))


def _extract_top(s, payload, n_out):
    nrows = s.shape[0]
    rid = lax.broadcasted_iota(jnp.int32, s.shape, 0).astype(F32)
    vals, pays = [], []
    for _ in range(n_out):
        mx = jnp.max(s, axis=0, keepdims=True)
        pos = jnp.min(jnp.where(s == mx, rid, float(nrows)), axis=0, keepdims=True)
        hit = rid == pos
        vals.append(mx)
        if payload is None:
            pays.append(pos)
        else:
            pays.append(jnp.sum(jnp.where(hit, payload, 0), axis=0, keepdims=True))
        s = jnp.where(hit, -jnp.inf, s)
    pays = jnp.concatenate(pays, axis=0)
    return jnp.concatenate(vals, axis=0), (pays.astype(jnp.int32) if payload is None else pays)


def _pair_candidates(s1, i1, s2, i2):
    k = PEER_TOPK
    sub = lax.broadcasted_iota(jnp.int32, (8, s1.shape[1]), 0)
    cand, cidx = [], []
    for bt in range(k // 8):
        cand.append(s1[0:1, :] + s2[bt * 8:(bt + 1) * 8, :])
        cidx.append(i1[0:1, :] * PEER_NKEYS + i2[bt * 8:(bt + 1) * 8, :])
    for a in range(1, 8):
        bmax = k // (a + 1) - 1
        c = s1[a:a + 1, :] + s2[0:8, :]
        cand.append(c if bmax >= 7 else jnp.where(sub <= bmax, c, -jnp.inf))
        cidx.append(i1[a:a + 1, :] * PEER_NKEYS + i2[0:8, :])
    cand.append(s1[8:k, :] + s2[0:1, :])
    cidx.append(i1[8:k, :] * PEER_NKEYS + i2[0:1, :])
    return jnp.concatenate(cand, axis=0), jnp.concatenate(cidx, axis=0)


def _peer_topk_kernel(h_ref, g_ref, sh_ref, sc_ref, wq_ref, keys_ref, hn_ref, idx_ref, gate_ref):
    hn = _adaln(h_ref[...], g_ref[...], sh_ref[0], sc_ref[0])
    hn_ref[...] = hn
    qt = lax.dot_general(wq_ref[...], hn.astype(BF16), (((1,), (1,)), ((), ())),
                         preferred_element_type=F32)
    idx_rows, gate_rows = [], []
    for hd in range(PEER_HEADS):
        top = []
        for p in range(2):
            hp = hd * 2 + p
            s = jnp.dot(keys_ref[hp], qt[hp * PEER_HALF:(hp + 1) * PEER_HALF, :],
                        preferred_element_type=F32, precision=HI)
            top.append(_extract_top(s, None, PEER_TOPK))
        (s1, i1), (s2, i2) = top
        cand, cidx = _pair_candidates(s1, i1, s2, i2)
        top_s, eidx = _extract_top(cand, cidx, PEER_TOPK)
        e = jnp.exp(top_s - top_s[0:1, :])
        gate_rows.append(e / jnp.sum(e, axis=0, keepdims=True))
        idx_rows.append(eidx)
    idx_ref[...] = jnp.concatenate(idx_rows, axis=0).T
    gate_ref[...] = jnp.concatenate(gate_rows, axis=0).T


def peer_topk(h, g, sh, sc, wq_t, keys, rows_per_mod, tm=256):
    m, d = h.shape
    tm = min(tm, m)
    nb = sh.shape[0]
    mod_map = lambda i: ((i * tm) // rows_per_mod, 0, 0)
    nq = wq_t.shape[0]
    return pl.pallas_call(
        _peer_topk_kernel,
        grid=(m // tm,),
        in_specs=[pl.BlockSpec((tm, d), lambda i: (i, 0)),
                  pl.BlockSpec((1, d), lambda i: (0, 0)),
                  pl.BlockSpec((1, 1, d), mod_map),
                  pl.BlockSpec((1, 1, d), mod_map),
                  pl.BlockSpec((nq, d), lambda i: (0, 0)),
                  pl.BlockSpec(keys.shape, lambda i: (0, 0, 0))],
        out_specs=[pl.BlockSpec((tm, d), lambda i: (i, 0)),
                   pl.BlockSpec((tm, PEER_SEL), lambda i: (i, 0)),
                   pl.BlockSpec((tm, PEER_SEL), lambda i: (i, 0))],
        out_shape=[jax.ShapeDtypeStruct((m, d), F32),
                   jax.ShapeDtypeStruct((m, PEER_SEL), jnp.int32),
                   jax.ShapeDtypeStruct((m, PEER_SEL), F32)],
        compiler_params=_cparams(("parallel",)),
        name="peer_topk",
    )(h, g.reshape(1, d), sh.reshape(nb, 1, d), sc.reshape(nb, 1, d), wq_t, keys)


PEER_SLOTS = 8
PEER_AHEAD = PEER_SLOTS - 1
PEER_DMA_QUEUES = 2


def _unpack_pair(w):
    lo = lax.bitcast_convert_type(lax.shift_left(w, jnp.int32(16)), F32)
    hi = lax.bitcast_convert_type(lax.bitwise_and(w, jnp.int32(-65536)), F32)
    return lo, hi


def _pack_kernel(u_ref, v_ref, o_ref):
    te, d = u_ref.shape[1:]
    nch = d // 128
    half = nch // 2

    def words(ref, rows, r):
        lo = lax.bitcast_convert_type(ref[0, rows, r * 128:(r + 1) * 128].astype(BF16).astype(F32), jnp.int32)
        hi = lax.bitcast_convert_type(
            ref[0, rows, (r + half) * 128:(r + half + 1) * 128].astype(BF16).astype(F32), jnp.int32)
        return lax.bitwise_or(lax.bitwise_and(hi, jnp.int32(-65536)), lax.shift_right_logical(lo, jnp.int32(16)))

    for g in range(te // 8):
        rows = slice(g * 8, (g + 1) * 8)
        for r in range(half):
            o_ref[pl.ds(g * 8 * nch + r, 8, stride=nch), :] = words(u_ref, rows, r)
            o_ref[pl.ds(g * 8 * nch + half + r, 8, stride=nch), :] = words(v_ref, rows, r)


def pack_expert_table(u_all, v_all, layer, te=128):
    _, ne, d = u_all.shape
    nch = d // 128
    out = pl.pallas_call(
        _pack_kernel,
        grid=(ne // te,),
        in_specs=[pl.BlockSpec((1, te, d), lambda i: (layer, i, 0)),
                  pl.BlockSpec((1, te, d), lambda i: (layer, i, 0))],
        out_specs=pl.BlockSpec((te * nch, 128), lambda i: (i, 0)),
        out_shape=jax.ShapeDtypeStruct((ne * nch, 128), jnp.int32),
        compiler_params=_cparams(("parallel",)),
        name="pack_expert_table",
    )(u_all, v_all)
    return out.reshape(ne, nch, 128)


def _peer_gather_kernel(idx_hbm, gate_ref, x_ref, res_ref, gt_ref, tab_hbm, out_ref,
                        idx_smem, buf, sems, idx_sem, act_ref, zred_ref, xt_ref, yt_ref, *, tb):
    d = x_ref.shape[1]
    nrow = d // 128
    nvr = nrow // 8
    npk = nvr // 2
    blk = pl.program_id(0)

    idx_cp = pltpu.make_async_copy(idx_hbm.at[pl.ds(blk * (tb * PEER_SEL), tb * PEER_SEL)], idx_smem, idx_sem)
    idx_cp.start()
    idx_cp.wait()

    def start_row(base, slot, k):
        pltpu.make_async_copy(tab_hbm.at[idx_smem[base + k]], buf.at[slot, k], sems.at[slot]).start(
            priority=k % PEER_DMA_QUEUES)

    def issue(t, slot):
        for k in range(PEER_SEL):
            start_row(t * PEER_SEL, slot, k)

    def issuer(t, slot):
        nxt = [0]

        def pump(n):
            for _ in range(n):
                if nxt[0] < PEER_SEL:
                    start_row(t * PEER_SEL, slot, nxt[0])
                    nxt[0] += 1
        return pump

    def wait(slot):
        pltpu.make_async_copy(tab_hbm.at[pl.ds(0, PEER_SEL)], buf.at[slot], sems.at[slot]).wait()

    last = tb - 1
    no_pump = lambda n: None

    for t0 in range(PEER_AHEAD):
        issue(t0, t0)

    def to_tiles(g, carry):
        r0 = pl.multiple_of(g * 8, 8)
        for c in range(nrow):
            xt_ref[pl.ds(g * (8 * nrow) + c, 8, stride=nrow), :] = x_ref[pl.ds(r0, 8), c * 128:(c + 1) * 128]
        return carry

    lax.fori_loop(0, tb // 8, to_tiles, 0)

    def reduce_phase(t, slot, zslot, pump):
        xs = [xt_ref[pl.ds(pl.multiple_of(t * nrow + r * 8, 8), 8), :] for r in range(nvr)]
        lane = lax.broadcasted_iota(jnp.int32, (8, 128), 1)
        nacc = 4
        accs = [jnp.zeros((8, 128), F32) for _ in range(nacc)]
        for k in range(PEER_SEL):
            p = None
            for j in range(npk):
                lo, hi = _unpack_pair(buf[slot, k, j * 8:(j + 1) * 8, :])
                q = lo * xs[j] + hi * xs[j + npk]
                p = q if p is None else p + q
            row_sums = jnp.sum(p, axis=1, keepdims=True)
            accs[k % nacc] = jnp.where(lane == k, row_sums, accs[k % nacc])
            if k % 8 in (2, 6):
                pump(1)
        zred_ref[zslot] = (accs[0] + accs[1]) + (accs[2] + accs[3])

    def act_values(t, zslot):
        ones = jnp.ones((128, 128), F32)
        diag = (lax.broadcasted_iota(jnp.int32, (128, 128), 0)
                == lax.broadcasted_iota(jnp.int32, (128, 128), 1))
        score = jnp.sum(zred_ref[zslot], axis=0, keepdims=True)
        gates = gate_ref[pl.ds(pl.multiple_of((t // 8) * 8, 8), 8), :]
        sub = lax.broadcasted_iota(jnp.int32, (8, 128), 0)
        gate = jnp.sum(jnp.where(sub == t % 8, gates, 0.0), axis=0, keepdims=True)
        act = _gelu(score) * gate
        return jnp.dot(jnp.where(diag, act, 0.0), ones, preferred_element_type=F32, precision=HI)

    def mix_phase(t, slot, aslot, pump):
        nacc = 4
        accs = [[None] * nvr for _ in range(nacc)]
        for k in range(PEER_SEL):
            a = jnp.broadcast_to(act_ref[aslot, k:k + 1, :], (8, 128))
            for j in range(npk):
                pair = _unpack_pair(buf[slot, k, (npk + j) * 8:(npk + j + 1) * 8, :])
                for r, val in ((j, pair[0]), (j + npk, pair[1])):
                    term = a * val
                    prev = accs[k % nacc][r]
                    accs[k % nacc][r] = term if prev is None else prev + term
            if k % 2 == 0 and k < 96:
                pump(1)
        for r in range(nvr):
            tot = accs[0][r]
            for j in range(1, nacc):
                tot = tot + accs[j][r]
            yt_ref[pl.ds(pl.multiple_of(t * nrow + r * 8, 8), 8), :] = tot

    assert PEER_AHEAD >= 3
    wait(0)
    reduce_phase(0, 0, 0, no_pump)
    act_ref[0] = act_values(0, 0)
    wait(1)
    reduce_phase(1, 1, 1, no_pump)

    def trip(t, carry):
        ta = t + PEER_AHEAD
        pump = issuer(jnp.minimum(ta, last), ta % PEER_SLOTS)
        t1 = t + 1
        act_next = act_values(jnp.minimum(t1, last), t1 % 2)
        mix_phase(t, t % PEER_SLOTS, t % 2, pump)
        t2 = t + 2
        wait(t2 % PEER_SLOTS)
        reduce_phase(jnp.minimum(t2, last), t2 % PEER_SLOTS, t2 % 2, pump)
        pump(PEER_SEL)
        act_ref[t1 % 2] = act_next
        return carry

    lax.fori_loop(0, tb, trip, 0)
    for e in range(2, PEER_AHEAD):
        wait((tb + e) % PEER_SLOTS)

    def from_tiles(g, carry):
        r0 = pl.multiple_of(g * 8, 8)
        for c in range(nrow):
            cols = slice(c * 128, (c + 1) * 128)
            y = yt_ref[pl.ds(g * (8 * nrow) + c, 8, stride=nrow), :]
            out_ref[pl.ds(r0, 8), cols] = res_ref[pl.ds(r0, 8), cols] + gt_ref[0, :, cols] * y
        return carry

    lax.fori_loop(0, tb // 8, from_tiles, 0)


def peer_gather(idx, gate, x, res, gt, table, rows_per_mod, tb=512):
    m, d = x.shape
    tb = min(tb, m)
    nb = gt.shape[0]
    nrow = d // 128
    mod_map = lambda i: ((i * tb) // rows_per_mod, 0, 0)
    tok_spec = pl.BlockSpec((tb, d), lambda i: (i, 0))
    return pl.pallas_call(
        functools.partial(_peer_gather_kernel, tb=tb),
        grid=(m // tb,),
        in_specs=[pl.BlockSpec(memory_space=pl.ANY),
                  pl.BlockSpec((tb, PEER_SEL), lambda i: (i, 0)),
                  tok_spec,
                  tok_spec,
                  pl.BlockSpec((1, 1, d), mod_map),
                  pl.BlockSpec(memory_space=pl.ANY)],
        out_specs=tok_spec,
        out_shape=jax.ShapeDtypeStruct((m, d), F32),
        scratch_shapes=[pltpu.SMEM((tb * PEER_SEL,), jnp.int32),
                        pltpu.VMEM((PEER_SLOTS, PEER_SEL, nrow, 128), jnp.int32),
                        pltpu.SemaphoreType.DMA((PEER_SLOTS,)),
                        pltpu.SemaphoreType.DMA(()),
                        pltpu.VMEM((2, PEER_SEL, 128), F32),
                        pltpu.VMEM((2, 8, 128), F32),
                        pltpu.VMEM((tb * nrow, 128), F32),
                        pltpu.VMEM((tb * nrow, 128), F32)],
        compiler_params=_cparams(("arbitrary",)),
        name="peer_gather",
    )(idx.reshape(-1), gate, x, res, gt.reshape(nb, 1, d), table)


def _rmsnorm_kernel(x_ref, g_ref, o_ref):
    x = x_ref[...]
    o_ref[...] = x * lax.rsqrt(jnp.mean(x * x, axis=-1, keepdims=True) + EPS) * g_ref[...]


def rmsnorm(x, g, tm=512):
    m, d = x.shape
    return pl.pallas_call(
        _rmsnorm_kernel,
        grid=(m // tm,),
        in_specs=[pl.BlockSpec((tm, d), lambda i: (i, 0)), pl.BlockSpec((1, d), lambda i: (0, 0))],
        out_specs=pl.BlockSpec((tm, d), lambda i: (i, 0)),
        out_shape=jax.ShapeDtypeStruct((m, d), F32),
        compiler_params=_cparams(("parallel",)),
        name="final_rmsnorm",
    )(x, g.reshape(1, d))


def _split_gates(gcol, grow, bsz, t):
    gc = gcol.reshape(bsz, t, 4, ML_HEADS).transpose(0, 3, 1, 2)
    gr = grow.reshape(4, ML_HEADS, bsz, t).transpose(2, 1, 0, 3)
    return gc, gr


def _mlstm_layer(h_lat, h_ctx, mods, cmods, g_mix, w_in, conv_w, conv_b, gate_b, head_g, w_out):
    bsz, t, d = h_lat.shape
    tc = h_ctx.shape[1]
    sh1, sc1, gt1 = mods
    csh1, csc1 = cmods
    qk_w = 2 * ML_HEADS * ML_DQK
    n_main = qk_w + 2 * ML_HEADS * ML_DV
    w_main = w_in[:, :n_main].astype(BF16)
    w_g = w_in[:, n_main:]
    k_scale = jnp.concatenate([jnp.ones((qk_w // 2,), F32), jnp.full((qk_w // 2,), ML_DQK ** -0.5, F32)])
    sw = ML_DV + ML_NX

    xc = h_ctx.reshape(bsz * tc, d)
    projc = ln_matmul(xc, g_mix, csh1, csc1, w_main, bsz * tc).reshape(bsz, tc, n_main)
    gcc, grc = gate_preacts(xc, g_mix, csh1, csc1, w_g, gate_b, bsz * tc)
    gcc, grc = _split_gates(gcc, grc, bsz, tc)
    qkc = conv_silu(projc, conv_w, conv_b, k_scale, qk_w, on_grid=False)
    s0 = jnp.zeros((bsz, ML_HEADS, ML_DQK, sw), F32)
    m0 = jnp.zeros((bsz, ML_HEADS, 8, 128), F32)
    _, sf, mf = mlstm_scan(qkc, projc, gcc, grc, s0, m0, rev=False, emit=False)
    _, sb, mb = mlstm_scan(qkc, projc, gcc, grc, s0, m0, rev=True, emit=False)

    x2 = h_lat.reshape(bsz * t, d)
    proj = ln_matmul(x2, g_mix, sh1, sc1, w_main, t).reshape(bsz, t, n_main)
    gcl, grl = gate_preacts(x2, g_mix, sh1, sc1, w_g, gate_b, t)
    gcl, grl = _split_gates(gcl, grl, bsz, t)
    qk = conv_silu(proj, conv_w, conv_b, k_scale, qk_w, on_grid=True)
    h_f, _, _ = mlstm_scan(qk, proj, gcl, grl, sf, mf, rev=False, emit=True)
    h_b, _, _ = mlstm_scan(qk, proj, gcl, grl, sb, mb, rev=True, emit=True)
    hw = ML_HEADS * ML_DV
    out = mlstm_out(h_f.reshape(bsz * t, hw), h_b.reshape(bsz * t, hw), proj.reshape(bsz * t, n_main),
                    head_g.reshape(-1), w_out.astype(BF16), x2, gt1, t)
    return out


def _chunk_mlp_layer(x2, t, mods, g_mix, w_in, g_v, w_s, b_s, w_out):
    sh1, sc1, gt1 = mods
    z = ln_matmul(x2, g_mix, sh1, sc1, w_in.astype(BF16), t, act="gelu")
    return chunk_mlp_out(z, g_v, w_s.astype(BF16), b_s, w_out.astype(BF16), x2, gt1, t)


def _peer_layer(x2, t, mods, g_chan, w_q, keys, u_all, v_all, layer):
    sh2, sc2, gt2 = mods
    hn, idx, gate = peer_topk(x2, g_chan, sh2, sc2, w_q.T.astype(BF16),
                              keys.reshape(PEER_HEADS * 2, PEER_NKEYS, PEER_HALF), t)
    return peer_gather(idx, gate, hn, x2, gt2, pack_expert_table(u_all, v_all, layer), t)


def kernel(x, c, ctx, c_ctx, w_mod, b_mod, g_mix, g_chan, ml_w_in, ml_conv_w, ml_conv_b, ml_gate_b, ml_head_g, ml_w_out, cm_w_in, cm_g_v, cm_w_s, cm_b_s, cm_w_out, peer_w_q, peer_keys, peer_u, peer_v, g_final):
    bsz, t, d = x.shape
    depth = w_mod.shape[0]
    cond = jnp.zeros((8, d), F32).at[:bsz].set(c).at[bsz].set(c_ctx)
    mods = modulation(cond, w_mod, b_mod)
    h2 = x.reshape(bsz * t, d)
    for i in range(depth):
        j = i // 2
        mi = mods[i].reshape(8, 6, d)
        lat = [mi[:bsz, s] for s in range(6)]
        cx = [mi[bsz:bsz + 1, s] for s in range(6)]
        if i % 2 == 0:
            h2 = _mlstm_layer(h2.reshape(bsz, t, d), ctx, lat[:3], cx[:2], g_mix[i], ml_w_in[j], ml_conv_w[j],
                              ml_conv_b[j], ml_gate_b[j], ml_head_g[j], ml_w_out[j])
        else:
            h2 = _chunk_mlp_layer(h2, t, lat[:3], g_mix[i], cm_w_in[j], cm_g_v[j], cm_w_s[j], cm_b_s[j],
                                  cm_w_out[j])
        h2 = _peer_layer(h2, t, lat[3:], g_chan[i], peer_w_q[i], peer_keys[i], peer_u, peer_v, i)
    return rmsnorm(h2, g_final).reshape(bsz, t, d)
```

```python
import functools

import jax
import jax.numpy as jnp
from jax import lax
from jax.experimental import pallas as pl
from jax.experimental.pallas import tpu as pltpu

F32 = jnp.float32
BF16 = jnp.bfloat16
EPS = 1e-6

GRID_W = 64
ML_HEADS = 4
ML_DQK = 256
ML_DV = 512
ML_CHUNK = 128
ML_NX = 128
CM_CHUNK = 128
CM_GROUPS = 8
PEER_HEADS = 8
PEER_NKEYS = 128
PEER_HALF = 128
PEER_TOPK = 16
PEER_SEL = PEER_HEADS * PEER_TOPK

VMEM_LIMIT = 48 * 1024 * 1024
HI = lax.Precision.HIGHEST


def _cparams(sem):
    return pltpu.CompilerParams(dimension_semantics=sem, vmem_limit_bytes=VMEM_LIMIT)


def _sigmoid(x):
    return 1.0 / (1.0 + jnp.exp(-x))


def _gelu(x):
    return 0.5 * x * (1.0 + jnp.tanh(0.7978845608028654 * (x + 0.044715 * (x * x * x))))


def _log_sigmoid(x):
    return jnp.minimum(x, 0.0) - jnp.log(1.0 + jnp.exp(-jnp.abs(x)))


def _adaln(x, g, sh, sc):
    ms = jnp.mean(x * x, axis=-1, keepdims=True)
    return (x * lax.rsqrt(ms + EPS) * g) * (1.0 + sc) + sh


def _mod_kernel(c_ref, w_ref, b_ref, o_ref):
    a = c_ref[...]
    a = a * _sigmoid(a)
    o_ref[0] = jnp.dot(a, w_ref[0], preferred_element_type=F32) + b_ref[0]


def modulation(cond, w_mod, b_mod, tn=1024):
    nl, d, n = w_mod.shape
    return pl.pallas_call(
        _mod_kernel,
        grid=(nl, n // tn),
        in_specs=[pl.BlockSpec((8, d), lambda l, j: (0, 0)),
                  pl.BlockSpec((1, d, tn), lambda l, j: (l, 0, j)),
                  pl.BlockSpec((1, 1, tn), lambda l, j: (l, 0, j))],
        out_specs=pl.BlockSpec((1, 8, tn), lambda l, j: (l, 0, j)),
        out_shape=jax.ShapeDtypeStruct((nl, 8, n), F32),
        compiler_params=_cparams(("parallel", "parallel")),
        name="modulation",
    )(cond, w_mod, b_mod.reshape(nl, 1, n))


def _ln_mm_kernel(x_ref, g_ref, sh_ref, sc_ref, w_ref, o_ref, hn_ref, *, act):
    @pl.when(pl.program_id(1) == 0)
    def _():
        hn_ref[...] = _adaln(x_ref[...], g_ref[...], sh_ref[0], sc_ref[0]).astype(BF16)

    r = jnp.dot(hn_ref[...], w_ref[...], preferred_element_type=F32)
    if act == "gelu":
        r = _gelu(r)
    o_ref[...] = r.astype(o_ref.dtype)


def ln_matmul(x, g, sh, sc, w, rows_per_mod, act=None, tm=512, tn=1024):
    m, d = x.shape
    n = w.shape[1]
    tm = min(tm, m)
    nb = sh.shape[0]
    mod_map = lambda i, j: ((i * tm) // rows_per_mod, 0, 0)
    return pl.pallas_call(
        functools.partial(_ln_mm_kernel, act=act),
        grid=(m // tm, n // tn),
        in_specs=[pl.BlockSpec((tm, d), lambda i, j: (i, 0)),
                  pl.BlockSpec((1, d), lambda i, j: (0, 0)),
                  pl.BlockSpec((1, 1, d), mod_map),
                  pl.BlockSpec((1, 1, d), mod_map),
                  pl.BlockSpec((d, tn), lambda i, j: (0, j))],
        out_specs=pl.BlockSpec((tm, tn), lambda i, j: (i, j)),
        out_shape=jax.ShapeDtypeStruct((m, n), BF16),
        scratch_shapes=[pltpu.VMEM((tm, d), BF16)],
        compiler_params=_cparams(("parallel", "arbitrary")),
        name="ln_matmul",
    )(x, g.reshape(1, d), sh.reshape(nb, 1, d), sc.reshape(nb, 1, d), w)


def _gates_kernel(x_ref, g_ref, sh_ref, sc_ref, w_ref, wt_ref, b_ref, bt_ref, oc_ref, or_ref):
    hn = _adaln(x_ref[...], g_ref[...], sh_ref[0], sc_ref[0])
    oc_ref[...] = jnp.dot(hn, w_ref[...], preferred_element_type=F32) + b_ref[...]
    or_ref[...] = lax.dot_general(wt_ref[...], hn, (((1,), (1,)), ((), ())),
                                  preferred_element_type=F32) + bt_ref[...]


def gate_preacts(x, g, sh, sc, w_g, b_g, rows_per_mod, tm=512):
    m, d = x.shape
    ng = w_g.shape[1]
    tm = min(tm, m)
    nb = sh.shape[0]
    mod_map = lambda i: ((i * tm) // rows_per_mod, 0, 0)
    return pl.pallas_call(
        _gates_kernel,
        grid=(m // tm,),
        in_specs=[pl.BlockSpec((tm, d), lambda i: (i, 0)),
                  pl.BlockSpec((1, d), lambda i: (0, 0)),
                  pl.BlockSpec((1, 1, d), mod_map),
                  pl.BlockSpec((1, 1, d), mod_map),
                  pl.BlockSpec((d, ng), lambda i: (0, 0)),
                  pl.BlockSpec((ng, d), lambda i: (0, 0)),
                  pl.BlockSpec((1, ng), lambda i: (0, 0)),
                  pl.BlockSpec((ng, 1), lambda i: (0, 0))],
        out_specs=[pl.BlockSpec((tm, ng), lambda i: (i, 0)),
                   pl.BlockSpec((ng, tm), lambda i: (0, i))],
        out_shape=[jax.ShapeDtypeStruct((m, ng), F32),
                   jax.ShapeDtypeStruct((ng, m), F32)],
        compiler_params=_cparams(("parallel",)),
        name="gate_preacts",
    )(x, g.reshape(1, d), sh.reshape(nb, 1, d), sc.reshape(nb, 1, d),
      w_g, w_g.T, b_g.reshape(1, ng), b_g.reshape(ng, 1))


def _conv_kernel(*refs, on_grid, tt, nt):
    if on_grid:
        xp_ref, xc_ref, xn_ref, w_ref, b_ref, s_ref, o_ref = refs
    else:
        xc_ref, w_ref, b_ref, s_ref, o_ref = refs
    i = pl.program_id(1)
    ct = xc_ref.shape[-1]
    cur = xc_ref[0].astype(F32)
    tok = lax.broadcasted_iota(jnp.int32, (tt, ct), 0)
    if on_grid:
        first = (tok % GRID_W) == 0
        last = (tok % GRID_W) == GRID_W - 1
    else:
        first = tok == 0
        last = tok == tt - 1

    def row_taps(base, dr):
        left = jnp.where(first, 0.0, pltpu.roll(base, 1, axis=0))
        right = jnp.where(last, 0.0, pltpu.roll(base, tt - 1, axis=0))
        return left * w_ref[dr, 0:1, :] + base * w_ref[dr, 1:2, :] + right * w_ref[dr, 2:3, :]

    y = row_taps(cur, 1)
    if on_grid:
        prev = jnp.where(i > 0, xp_ref[0].astype(F32), 0.0)
        nxt = jnp.where(i < nt - 1, xn_ref[0].astype(F32), 0.0)
        up = jnp.concatenate([prev, cur[:tt - GRID_W]], axis=0)
        down = jnp.concatenate([cur[GRID_W:], nxt], axis=0)
        y = y + row_taps(up, 0) + row_taps(down, 2)
    y = y + b_ref[...]
    y = y * _sigmoid(y)
    o_ref[0] = (y * s_ref[...]).astype(o_ref.dtype)


def conv_silu(proj, conv_w, conv_b, out_scale, n_ch, on_grid, tt=512, ct=512):
    bsz, t, _ = proj.shape
    if not on_grid:
        tt = t
    nt = t // tt
    rpt = tt // GRID_W
    kern = functools.partial(_conv_kernel, on_grid=on_grid, tt=tt, nt=nt)
    cur_spec = pl.BlockSpec((1, tt, ct), lambda b, i, j: (b, i, j))
    par_specs = [pl.BlockSpec((3, 3, ct), lambda b, i, j: (0, 0, j)),
                 pl.BlockSpec((1, ct), lambda b, i, j: (0, j)),
                 pl.BlockSpec((1, ct), lambda b, i, j: (0, j))]
    if on_grid:
        nrow = t // GRID_W
        in_specs = [pl.BlockSpec((1, GRID_W, ct), lambda b, i, j: (b, jnp.maximum(i * rpt - 1, 0), j)),
                    cur_spec,
                    pl.BlockSpec((1, GRID_W, ct), lambda b, i, j: (b, jnp.minimum((i + 1) * rpt, nrow - 1), j))]
        args = (proj, proj, proj)
    else:
        in_specs = [cur_spec]
        args = (proj,)
    return pl.pallas_call(
        kern,
        grid=(bsz, nt, n_ch // ct),
        in_specs=in_specs + par_specs,
        out_specs=pl.BlockSpec((1, tt, ct), lambda b, i, j: (b, i, j)),
        out_shape=jax.ShapeDtypeStruct((bsz, t, n_ch), BF16),
        compiler_params=_cparams(("parallel", "parallel", "parallel")),
        name="conv_silu_grid" if on_grid else "conv_silu_seq",
    )(*args, conv_w, conv_b.reshape(1, n_ch), out_scale.reshape(1, n_ch))


def _mlstm_chunk(q_ref, k_ref, v_ref, gc_ref, gr_ref, h_ref, s_scr, m_scr, *, rev, emit):
    ln = ML_CHUNK
    gi = 2 if rev else 0
    gc = gc_ref[0, 0]
    gr = gr_ref[0, 0]
    i_col = gc[:, gi:gi + 1]
    f_col = _log_sigmoid(gc[:, gi + 1:gi + 2])
    i_row = gr[gi:gi + 1, :]
    f_row = _log_sigmoid(gr[gi + 1:gi + 2, :])
    row = lax.broadcasted_iota(jnp.int32, (ln, ln), 0)
    col = lax.broadcasted_iota(jnp.int32, (ln, ln), 1)
    seen = (col >= row) if rev else (col <= row)
    upto = (row >= col) if rev else (row <= col)
    b_col = jnp.sum(jnp.where(seen, f_row, 0.0), axis=1, keepdims=True)
    b_row = jnp.sum(jnp.where(upto, f_col, 0.0), axis=0, keepdims=True)
    b_end = jnp.sum(f_row, axis=1, keepdims=True)
    m_old = m_scr[0:1, 0:1]
    w_log = b_end - b_row + i_row
    m_new = jnp.maximum(b_end + m_old, jnp.max(w_log, axis=1, keepdims=True))
    a_state = jnp.exp(b_end + m_old - m_new)
    w_col = jnp.exp(b_end - b_col + i_col - m_new)

    kc = k_ref[0]
    vc = v_ref[0]
    lane = lax.broadcasted_iota(jnp.int32, (ln, ML_NX), 1)
    ones_col = jnp.where(lane == 0, 1.0, 0.0)
    vx = jnp.concatenate([vc.astype(F32), ones_col], axis=1)
    s_old = s_scr[...]

    if emit:
        qc = q_ref[0]
        inter = b_col + m_old
        d_log = jnp.where(seen, b_col - b_row + i_row, -jnp.inf)
        m_t = jnp.maximum(inter, jnp.max(d_log, axis=1, keepdims=True))
        a_inter = jnp.exp(inter - m_t)
        qk = lax.dot_general(qc, kc, (((1,), (1,)), ((), ())), preferred_element_type=F32)
        p = jnp.exp(d_log - m_t) * qk
        num = a_inter * jnp.dot(qc, s_old.astype(BF16), preferred_element_type=F32) \
            + jnp.dot(p.astype(BF16), vx.astype(BF16), preferred_element_type=F32)
        den = num[:, ML_DV:ML_DV + 1]
        h_ref[0] = num[:, :ML_DV] / jnp.maximum(jnp.abs(den), jnp.exp(-m_t))

    vw = (w_col * vx).astype(BF16)
    kt = kc.astype(F32).T.astype(BF16)
    s_scr[...] = a_state * s_old + jnp.dot(kt, vw, preferred_element_type=F32)
    m_scr[...] = jnp.broadcast_to(m_new, m_scr.shape)


def _mlstm_kernel(*refs, emit, nc):
    n_in = 5 if emit else 4
    ins_f = refs[:n_in] if emit else (None,) + refs[:n_in]
    ins_b = refs[n_in:2 * n_in] if emit else (None,) + refs[n_in:2 * n_in]
    s0f_ref, m0f_ref, s0b_ref, m0b_ref = refs[2 * n_in:2 * n_in + 4]
    rest = refs[2 * n_in + 4:]
    if emit:
        hf_ref, hb_ref = rest[:2]
        rest = rest[2:]
    else:
        hf_ref = hb_ref = None
    sof_ref, mof_ref, sob_ref, mob_ref, sf_scr, mf_scr, sb_scr, mb_scr = rest
    c = pl.program_id(2)

    @pl.when(c == 0)
    def _():
        sf_scr[...] = s0f_ref[0, 0]
        mf_scr[...] = m0f_ref[0, 0]
        sb_scr[...] = s0b_ref[0, 0]
        mb_scr[...] = m0b_ref[0, 0]

    _mlstm_chunk(*ins_f, hf_ref, sf_scr, mf_scr, rev=False, emit=emit)
    _mlstm_chunk(*ins_b, hb_ref, sb_scr, mb_scr, rev=True, emit=emit)

    @pl.when(c == nc - 1)
    def _():
        sof_ref[0, 0] = sf_scr[...]
        mof_ref[0, 0] = mf_scr[...]
        sob_ref[0, 0] = sb_scr[...]
        mob_ref[0, 0] = mb_scr[...]


def mlstm_scan(qk_act, proj, gcol, grow, s0f, m0f, s0b, m0b, emit):
    bsz, t, _ = qk_act.shape
    nc = t // ML_CHUNK
    sw = ML_DV + ML_NX
    kq_blocks = ML_HEADS
    v_blocks = (2 * ML_HEADS * ML_DQK) // ML_DV
    state_specs = [pl.BlockSpec((1, 1, ML_DQK, sw), lambda b, h, c: (b, h, 0, 0)),
                   pl.BlockSpec((1, 1, 8, 128), lambda b, h, c: (b, h, 0, 0))]
    state_shapes = [jax.ShapeDtypeStruct((bsz, ML_HEADS, ML_DQK, sw), F32),
                    jax.ShapeDtypeStruct((bsz, ML_HEADS, 8, 128), F32)]

    def dir_specs(cm):
        specs = [pl.BlockSpec((1, ML_CHUNK, ML_DQK), lambda b, h, c: (b, cm(c), kq_blocks + h)),
                 pl.BlockSpec((1, ML_CHUNK, ML_DV), lambda b, h, c: (b, cm(c), v_blocks + h)),
                 pl.BlockSpec((1, 1, ML_CHUNK, 4), lambda b, h, c: (b, h, cm(c), 0)),
                 pl.BlockSpec((1, 1, 4, ML_CHUNK), lambda b, h, c: (b, h, 0, cm(c)))]
        args = [qk_act, proj, gcol, grow]
        if emit:
            specs = [pl.BlockSpec((1, ML_CHUNK, ML_DQK), lambda b, h, c: (b, cm(c), h))] + specs
            args = [qk_act] + args
        return specs, args

    fwd = lambda c: c
    bwd = lambda c: nc - 1 - c
    spec_f, arg_f = dir_specs(fwd)
    spec_b, arg_b = dir_specs(bwd)
    out_specs = state_specs + state_specs
    out_shape = state_shapes + state_shapes
    if emit:
        out_specs = [pl.BlockSpec((1, ML_CHUNK, ML_DV), lambda b, h, c: (b, fwd(c), h)),
                     pl.BlockSpec((1, ML_CHUNK, ML_DV), lambda b, h, c: (b, bwd(c), h))] + out_specs
        out_shape = [jax.ShapeDtypeStruct((bsz, t, ML_HEADS * ML_DV), F32)] * 2 + out_shape
    outs = pl.pallas_call(
        functools.partial(_mlstm_kernel, emit=emit, nc=nc),
        grid=(bsz, ML_HEADS, nc),
        in_specs=spec_f + spec_b + state_specs + state_specs,
        out_specs=out_specs,
        out_shape=out_shape,
        scratch_shapes=[pltpu.VMEM((ML_DQK, sw), F32), pltpu.VMEM((8, 128), F32),
                        pltpu.VMEM((ML_DQK, sw), F32), pltpu.VMEM((8, 128), F32)],
        compiler_params=_cparams(("parallel", "parallel", "arbitrary")),
        name="mlstm_emit" if emit else "mlstm_state",
    )(*arg_f, *arg_b, s0f, m0f, s0b, m0b)
    if emit:
        return (outs[0], outs[1]) + tuple(outs[2:])
    return (None, None) + tuple(outs)


def _ml_out_kernel(hf_ref, hb_ref, o_ref, hg_ref, w_ref, res_ref, gt_ref, out_ref, y_ref):
    @pl.when(pl.program_id(1) == 0)
    def _():
        h = hf_ref[...] + hb_ref[...]
        parts = []
        for hd in range(ML_HEADS):
            seg = h[:, hd * ML_DV:(hd + 1) * ML_DV]
            ms = jnp.mean(seg * seg, axis=-1, keepdims=True)
            parts.append(seg * lax.rsqrt(ms + EPS))
        hn = jnp.concatenate(parts, axis=1) * hg_ref[...]
        y_ref[...] = (hn * _sigmoid(o_ref[...].astype(F32))).astype(BF16)

    y = jnp.dot(y_ref[...], w_ref[...], preferred_element_type=F32)
    out_ref[...] = res_ref[...] + gt_ref[0] * y


def mlstm_out(h_f, h_b, proj, head_g, w_out, res, gt, rows_per_mod, tm=512, tn=1024):
    m, d = res.shape
    hw = h_f.shape[1]
    o_block = proj.shape[1] // hw - 1
    nb = gt.shape[0]
    mod_map = lambda i, j: ((i * tm) // rows_per_mod, 0, j)
    return pl.pallas_call(
        _ml_out_kernel,
        grid=(m // tm, d // tn),
        in_specs=[pl.BlockSpec((tm, hw), lambda i, j: (i, 0)),
                  pl.BlockSpec((tm, hw), lambda i, j: (i, 0)),
                  pl.BlockSpec((tm, hw), lambda i, j: (i, o_block)),
                  pl.BlockSpec((1, hw), lambda i, j: (0, 0)),
                  pl.BlockSpec((hw, tn), lambda i, j: (0, j)),
                  pl.BlockSpec((tm, tn), lambda i, j: (i, j)),
                  pl.BlockSpec((1, 1, tn), mod_map)],
        out_specs=pl.BlockSpec((tm, tn), lambda i, j: (i, j)),
        out_shape=jax.ShapeDtypeStruct((m, d), F32),
        scratch_shapes=[pltpu.VMEM((tm, hw), BF16)],
        compiler_params=_cparams(("parallel", "arbitrary")),
        name="mlstm_out",
    )(h_f, h_b, proj, head_g.reshape(1, hw), w_out, res, gt.reshape(nb, 1, d))


def _cm_out_kernel(z_ref, gv_ref, ws_ref, bs_ref, w_ref, res_ref, gt_ref, out_ref, y_ref, *, tm):
    width = gv_ref.shape[-1]
    gdim = width // CM_GROUPS

    @pl.when(pl.program_id(1) == 0)
    def _():
        for ch in range(tm // CM_CHUNK):
            rows = slice(ch * CM_CHUNK, (ch + 1) * CM_CHUNK)
            v = z_ref[rows, width:].astype(F32)
            v = v * lax.rsqrt(jnp.mean(v * v, axis=-1, keepdims=True) + EPS) * gv_ref[...]
            for g in range(CM_GROUPS):
                cols = slice(g * gdim, (g + 1) * gdim)
                sv = jnp.dot(ws_ref[g], v[:, cols].astype(BF16), preferred_element_type=F32) \
                    + bs_ref[:, g:g + 1]
                y_ref[rows, cols] = (z_ref[rows, cols].astype(F32) * sv).astype(BF16)

    y = jnp.dot(y_ref[...], w_ref[...], preferred_element_type=F32)
    out_ref[...] = res_ref[...] + gt_ref[0] * y


def chunk_mlp_out(z, g_v, w_s, b_s, w_out, res, gt, rows_per_mod, tm=512, tn=1024):
    m, d = res.shape
    width = g_v.shape[0]
    nb = gt.shape[0]
    mod_map = lambda i, j: ((i * tm) // rows_per_mod, 0, j)
    return pl.pallas_call(
        functools.partial(_cm_out_kernel, tm=tm),
        grid=(m // tm, d // tn),
        in_specs=[pl.BlockSpec((tm, 2 * width), lambda i, j: (i, 0)),
                  pl.BlockSpec((1, width), lambda i, j: (0, 0)),
                  pl.BlockSpec((CM_GROUPS, CM_CHUNK, CM_CHUNK), lambda i, j: (0, 0, 0)),
                  pl.BlockSpec((CM_CHUNK, CM_GROUPS), lambda i, j: (0, 0)),
                  pl.BlockSpec((width, tn), lambda i, j: (0, j)),
                  pl.BlockSpec((tm, tn), lambda i, j: (i, j)),
                  pl.BlockSpec((1, 1, tn), mod_map)],
        out_specs=pl.BlockSpec((tm, tn), lambda i, j: (i, j)),
        out_shape=jax.ShapeDtypeStruct((m, d), F32),
        scratch_shapes=[pltpu.VMEM((tm, width), BF16)],
        compiler_params=_cparams(("parallel", "arbitrary")),
        name="chunk_mlp_out",
    )(z, g_v.reshape(1, width), w_s, b_s.T, w_out, res, gt.reshape(nb, 1, d))


def _extract_top(s, payload, n_out):
    nrows = s.shape[0]
    rid = lax.broadcasted_iota(jnp.int32, s.shape, 0).astype(F32)
    vals, pays = [], []
    for _ in range(n_out):
        mx = jnp.max(s, axis=0, keepdims=True)
        pos = jnp.min(jnp.where(s == mx, rid, float(nrows)), axis=0, keepdims=True)
        hit = rid == pos
        vals.append(mx)
        if payload is None:
            pays.append(pos)
        else:
            pays.append(jnp.sum(jnp.where(hit, payload, 0), axis=0, keepdims=True))
        s = jnp.where(hit, -jnp.inf, s)
    pays = jnp.concatenate(pays, axis=0)
    return jnp.concatenate(vals, axis=0), (pays.astype(jnp.int32) if payload is None else pays)


def _pair_candidates(s1, i1, s2, i2):
    k = PEER_TOPK
    sub = lax.broadcasted_iota(jnp.int32, (8, s1.shape[1]), 0)
    cand, cidx = [], []
    for bt in range(k // 8):
        cand.append(s1[0:1, :] + s2[bt * 8:(bt + 1) * 8, :])
        cidx.append(i1[0:1, :] * PEER_NKEYS + i2[bt * 8:(bt + 1) * 8, :])
    for a in range(1, 8):
        bmax = k // (a + 1) - 1
        c = s1[a:a + 1, :] + s2[0:8, :]
        cand.append(c if bmax >= 7 else jnp.where(sub <= bmax, c, -jnp.inf))
        cidx.append(i1[a:a + 1, :] * PEER_NKEYS + i2[0:8, :])
    cand.append(s1[8:k, :] + s2[0:1, :])
    cidx.append(i1[8:k, :] * PEER_NKEYS + i2[0:1, :])
    return jnp.concatenate(cand, axis=0), jnp.concatenate(cidx, axis=0)


def _peer_topk_kernel(h_ref, g_ref, sh_ref, sc_ref, wq_ref, keys_ref, hn_ref, idx_ref, gate_ref):
    hn = _adaln(h_ref[...], g_ref[...], sh_ref[0], sc_ref[0])
    hn_ref[...] = hn
    qt = lax.dot_general(wq_ref[...], hn.astype(BF16), (((1,), (1,)), ((), ())),
                         preferred_element_type=F32)
    idx_rows, gate_rows = [], []
    for hd in range(PEER_HEADS):
        top = []
        for p in range(2):
            hp = hd * 2 + p
            s = jnp.dot(keys_ref[hp], qt[hp * PEER_HALF:(hp + 1) * PEER_HALF, :],
                        preferred_element_type=F32, precision=HI)
            top.append(_extract_top(s, None, PEER_TOPK))
        (s1, i1), (s2, i2) = top
        cand, cidx = _pair_candidates(s1, i1, s2, i2)
        top_s, eidx = _extract_top(cand, cidx, PEER_TOPK)
        e = jnp.exp(top_s - top_s[0:1, :])
        gate_rows.append(e / jnp.sum(e, axis=0, keepdims=True))
        idx_rows.append(eidx)
    idx_ref[...] = jnp.concatenate(idx_rows, axis=0).T
    gate_ref[...] = jnp.concatenate(gate_rows, axis=0).T


def peer_topk(h, g, sh, sc, wq_t, keys, rows_per_mod, tm=256):
    m, d = h.shape
    tm = min(tm, m)
    nb = sh.shape[0]
    mod_map = lambda i: ((i * tm) // rows_per_mod, 0, 0)
    nq = wq_t.shape[0]
    return pl.pallas_call(
        _peer_topk_kernel,
        grid=(m // tm,),
        in_specs=[pl.BlockSpec((tm, d), lambda i: (i, 0)),
                  pl.BlockSpec((1, d), lambda i: (0, 0)),
                  pl.BlockSpec((1, 1, d), mod_map),
                  pl.BlockSpec((1, 1, d), mod_map),
                  pl.BlockSpec((nq, d), lambda i: (0, 0)),
                  pl.BlockSpec(keys.shape, lambda i: (0, 0, 0))],
        out_specs=[pl.BlockSpec((tm, d), lambda i: (i, 0)),
                   pl.BlockSpec((tm, PEER_SEL), lambda i: (i, 0)),
                   pl.BlockSpec((tm, PEER_SEL), lambda i: (i, 0))],
        out_shape=[jax.ShapeDtypeStruct((m, d), F32),
                   jax.ShapeDtypeStruct((m, PEER_SEL), jnp.int32),
                   jax.ShapeDtypeStruct((m, PEER_SEL), F32)],
        compiler_params=_cparams(("parallel",)),
        name="peer_topk",
    )(h, g.reshape(1, d), sh.reshape(nb, 1, d), sc.reshape(nb, 1, d), wq_t, keys)


PEER_SLOTS = 8
PEER_AHEAD = PEER_SLOTS - 1
PEER_DMA_QUEUES = 2


def _unpack_pair(w):
    lo = lax.bitcast_convert_type(lax.shift_left(w, jnp.int32(16)), F32)
    hi = lax.bitcast_convert_type(lax.bitwise_and(w, jnp.int32(-65536)), F32)
    return lo, hi


def _pack_kernel(u_ref, v_ref, o_ref):
    te, d = u_ref.shape[1:]
    nch = d // 128
    half = nch // 2

    def words(ref, rows, r):
        lo = lax.bitcast_convert_type(ref[0, rows, r * 128:(r + 1) * 128].astype(BF16).astype(F32), jnp.int32)
        hi = lax.bitcast_convert_type(
            ref[0, rows, (r + half) * 128:(r + half + 1) * 128].astype(BF16).astype(F32), jnp.int32)
        return lax.bitwise_or(lax.bitwise_and(hi, jnp.int32(-65536)), lax.shift_right_logical(lo, jnp.int32(16)))

    for g in range(te // 8):
        rows = slice(g * 8, (g + 1) * 8)
        for r in range(half):
            o_ref[pl.ds(g * 8 * nch + r, 8, stride=nch), :] = words(u_ref, rows, r)
            o_ref[pl.ds(g * 8 * nch + half + r, 8, stride=nch), :] = words(v_ref, rows, r)


def pack_expert_table(u_all, v_all, layer, te=128):
    _, ne, d = u_all.shape
    nch = d // 128
    out = pl.pallas_call(
        _pack_kernel,
        grid=(ne // te,),
        in_specs=[pl.BlockSpec((1, te, d), lambda i: (layer, i, 0)),
                  pl.BlockSpec((1, te, d), lambda i: (layer, i, 0))],
        out_specs=pl.BlockSpec((te * nch, 128), lambda i: (i, 0)),
        out_shape=jax.ShapeDtypeStruct((ne * nch, 128), jnp.int32),
        compiler_params=_cparams(("parallel",)),
        name="pack_expert_table",
    )(u_all, v_all)
    return out.reshape(ne, nch, 128)


def _peer_gather_kernel(idx_hbm, gate_ref, x_ref, res_ref, gt_ref, tab_hbm, out_ref,
                        idx_smem, buf, sems, idx_sem, act_ref, zred_ref, xt_ref, yt_ref, *, tb):
    d = x_ref.shape[1]
    nrow = d // 128
    nvr = nrow // 8
    npk = nvr // 2
    blk = pl.program_id(0)

    idx_cp = pltpu.make_async_copy(idx_hbm.at[pl.ds(blk * (tb * PEER_SEL), tb * PEER_SEL)], idx_smem, idx_sem)
    idx_cp.start()
    idx_cp.wait()

    def start_row(base, slot, k):
        pltpu.make_async_copy(tab_hbm.at[idx_smem[base + k]], buf.at[slot, k], sems.at[slot]).start(
            priority=k % PEER_DMA_QUEUES)

    def issue(t, slot):
        for k in range(PEER_SEL):
            start_row(t * PEER_SEL, slot, k)

    def issuer(t, slot):
        nxt = [0]

        def pump(n):
            for _ in range(n):
                if nxt[0] < PEER_SEL:
                    start_row(t * PEER_SEL, slot, nxt[0])
                    nxt[0] += 1
        return pump

    def wait(slot):
        pltpu.make_async_copy(tab_hbm.at[pl.ds(0, PEER_SEL)], buf.at[slot], sems.at[slot]).wait()

    last = tb - 1
    no_pump = lambda n: None

    for t0 in range(PEER_AHEAD):
        issue(t0, t0)

    def to_tiles(g, carry):
        r0 = pl.multiple_of(g * 8, 8)
        for c in range(nrow):
            xt_ref[pl.ds(g * (8 * nrow) + c, 8, stride=nrow), :] = x_ref[pl.ds(r0, 8), c * 128:(c + 1) * 128]
        return carry

    lax.fori_loop(0, tb // 8, to_tiles, 0)

    def reduce_phase(t, slot, zslot, pump):
        xs = [xt_ref[pl.ds(pl.multiple_of(t * nrow + r * 8, 8), 8), :] for r in range(nvr)]
        lane = lax.broadcasted_iota(jnp.int32, (8, 128), 1)
        nacc = 4
        accs = [jnp.zeros((8, 128), F32) for _ in range(nacc)]
        for k in range(PEER_SEL):
            p = None
            for j in range(npk):
                lo, hi = _unpack_pair(buf[slot, k, j * 8:(j + 1) * 8, :])
                q = lo * xs[j] + hi * xs[j + npk]
                p = q if p is None else p + q
            row_sums = jnp.sum(p, axis=1, keepdims=True)
            accs[k % nacc] = jnp.where(lane == k, row_sums, accs[k % nacc])
            if k % 8 in (2, 6):
                pump(1)
        zred_ref[zslot] = (accs[0] + accs[1]) + (accs[2] + accs[3])

    def act_values(t, zslot):
        ones = jnp.ones((128, 128), F32)
        diag = (lax.broadcasted_iota(jnp.int32, (128, 128), 0)
                == lax.broadcasted_iota(jnp.int32, (128, 128), 1))
        score = jnp.sum(zred_ref[zslot], axis=0, keepdims=True)
        gates = gate_ref[pl.ds(pl.multiple_of((t // 8) * 8, 8), 8), :]
        sub = lax.broadcasted_iota(jnp.int32, (8, 128), 0)
        gate = jnp.sum(jnp.where(sub == t % 8, gates, 0.0), axis=0, keepdims=True)
        act = _gelu(score) * gate
        return jnp.dot(jnp.where(diag, act, 0.0), ones, preferred_element_type=F32, precision=HI)

    def mix_phase(t, slot, aslot, pump):
        nacc = 4
        accs = [[None] * nvr for _ in range(nacc)]
        for k in range(PEER_SEL):
            a = jnp.broadcast_to(act_ref[aslot, k:k + 1, :], (8, 128))
            for j in range(npk):
                pair = _unpack_pair(buf[slot, k, (npk + j) * 8:(npk + j + 1) * 8, :])
                for r, val in ((j, pair[0]), (j + npk, pair[1])):
                    term = a * val
                    prev = accs[k % nacc][r]
                    accs[k % nacc][r] = term if prev is None else prev + term
            if k % 2 == 0 and k < 96:
                pump(1)
        for r in range(nvr):
            tot = accs[0][r]
            for j in range(1, nacc):
                tot = tot + accs[j][r]
            yt_ref[pl.ds(pl.multiple_of(t * nrow + r * 8, 8), 8), :] = tot

    assert PEER_AHEAD >= 3
    wait(0)
    reduce_phase(0, 0, 0, no_pump)
    act_ref[0] = act_values(0, 0)
    wait(1)
    reduce_phase(1, 1, 1, no_pump)

    def trip(t, carry):
        ta = t + PEER_AHEAD
        pump = issuer(jnp.minimum(ta, last), ta % PEER_SLOTS)
        t1 = t + 1
        act_next = act_values(jnp.minimum(t1, last), t1 % 2)
        mix_phase(t, t % PEER_SLOTS, t % 2, pump)
        t2 = t + 2
        wait(t2 % PEER_SLOTS)
        reduce_phase(jnp.minimum(t2, last), t2 % PEER_SLOTS, t2 % 2, pump)
        pump(PEER_SEL)
        act_ref[t1 % 2] = act_next
        return carry

    lax.fori_loop(0, tb, trip, 0)
    for e in range(2, PEER_AHEAD):
        wait((tb + e) % PEER_SLOTS)

    def from_tiles(g, carry):
        r0 = pl.multiple_of(g * 8, 8)
        for c in range(nrow):
            cols = slice(c * 128, (c + 1) * 128)
            y = yt_ref[pl.ds(g * (8 * nrow) + c, 8, stride=nrow), :]
            out_ref[pl.ds(r0, 8), cols] = res_ref[pl.ds(r0, 8), cols] + gt_ref[0, :, cols] * y
        return carry

    lax.fori_loop(0, tb // 8, from_tiles, 0)


def peer_gather(idx, gate, x, res, gt, table, rows_per_mod, tb=512):
    m, d = x.shape
    tb = min(tb, m)
    nb = gt.shape[0]
    nrow = d // 128
    mod_map = lambda i: ((i * tb) // rows_per_mod, 0, 0)
    tok_spec = pl.BlockSpec((tb, d), lambda i: (i, 0))
    return pl.pallas_call(
        functools.partial(_peer_gather_kernel, tb=tb),
        grid=(m // tb,),
        in_specs=[pl.BlockSpec(memory_space=pl.ANY),
                  pl.BlockSpec((tb, PEER_SEL), lambda i: (i, 0)),
                  tok_spec,
                  tok_spec,
                  pl.BlockSpec((1, 1, d), mod_map),
                  pl.BlockSpec(memory_space=pl.ANY)],
        out_specs=tok_spec,
        out_shape=jax.ShapeDtypeStruct((m, d), F32),
        scratch_shapes=[pltpu.SMEM((tb * PEER_SEL,), jnp.int32),
                        pltpu.VMEM((PEER_SLOTS, PEER_SEL, nrow, 128), jnp.int32),
                        pltpu.SemaphoreType.DMA((PEER_SLOTS,)),
                        pltpu.SemaphoreType.DMA(()),
                        pltpu.VMEM((2, PEER_SEL, 128), F32),
                        pltpu.VMEM((2, 8, 128), F32),
                        pltpu.VMEM((tb * nrow, 128), F32),
                        pltpu.VMEM((tb * nrow, 128), F32)],
        compiler_params=_cparams(("arbitrary",)),
        name="peer_gather",
    )(idx.reshape(-1), gate, x, res, gt.reshape(nb, 1, d), table)


def _rmsnorm_kernel(x_ref, g_ref, o_ref):
    x = x_ref[...]
    o_ref[...] = x * lax.rsqrt(jnp.mean(x * x, axis=-1, keepdims=True) + EPS) * g_ref[...]


def rmsnorm(x, g, tm=512):
    m, d = x.shape
    return pl.pallas_call(
        _rmsnorm_kernel,
        grid=(m // tm,),
        in_specs=[pl.BlockSpec((tm, d), lambda i: (i, 0)), pl.BlockSpec((1, d), lambda i: (0, 0))],
        out_specs=pl.BlockSpec((tm, d), lambda i: (i, 0)),
        out_shape=jax.ShapeDtypeStruct((m, d), F32),
        compiler_params=_cparams(("parallel",)),
        name="final_rmsnorm",
    )(x, g.reshape(1, d))


def _split_gates(gcol, grow, bsz, t):
    gc = gcol.reshape(bsz, t, 4, ML_HEADS).transpose(0, 3, 1, 2)
    gr = grow.reshape(4, ML_HEADS, bsz, t).transpose(2, 1, 0, 3)
    return gc, gr


def _mlstm_layer(h_lat, h_ctx, mods, cmods, g_mix, w_in, conv_w, conv_b, gate_b, head_g, w_out):
    bsz, t, d = h_lat.shape
    tc = h_ctx.shape[1]
    sh1, sc1, gt1 = mods
    csh1, csc1 = cmods
    qk_w = 2 * ML_HEADS * ML_DQK
    n_main = qk_w + 2 * ML_HEADS * ML_DV
    w_main = w_in[:, :n_main].astype(BF16)
    w_g = w_in[:, n_main:]
    k_scale = jnp.concatenate([jnp.ones((qk_w // 2,), F32), jnp.full((qk_w // 2,), ML_DQK ** -0.5, F32)])
    sw = ML_DV + ML_NX

    xc = h_ctx.reshape(bsz * tc, d)
    projc = ln_matmul(xc, g_mix, csh1, csc1, w_main, bsz * tc).reshape(bsz, tc, n_main)
    gcc, grc = gate_preacts(xc, g_mix, csh1, csc1, w_g, gate_b, bsz * tc)
    gcc, grc = _split_gates(gcc, grc, bsz, tc)
    qkc = conv_silu(projc, conv_w, conv_b, k_scale, qk_w, on_grid=False)
    s0 = jnp.zeros((bsz, ML_HEADS, ML_DQK, sw), F32)
    m0 = jnp.zeros((bsz, ML_HEADS, 8, 128), F32)
    _, _, sf, mf, sb, mb = mlstm_scan(qkc, projc, gcc, grc, s0, m0, s0, m0, emit=False)

    x2 = h_lat.reshape(bsz * t, d)
    proj = ln_matmul(x2, g_mix, sh1, sc1, w_main, t).reshape(bsz, t, n_main)
    gcl, grl = gate_preacts(x2, g_mix, sh1, sc1, w_g, gate_b, t)
    gcl, grl = _split_gates(gcl, grl, bsz, t)
    qk = conv_silu(proj, conv_w, conv_b, k_scale, qk_w, on_grid=True)
    h_f, h_b = mlstm_scan(qk, proj, gcl, grl, sf, mf, sb, mb, emit=True)[:2]
    hw = ML_HEADS * ML_DV
    out = mlstm_out(h_f.reshape(bsz * t, hw), h_b.reshape(bsz * t, hw), proj.reshape(bsz * t, n_main),
                    head_g.reshape(-1), w_out.astype(BF16), x2, gt1, t)
    return out


def _chunk_mlp_layer(x2, t, mods, g_mix, w_in, g_v, w_s, b_s, w_out):
    sh1, sc1, gt1 = mods
    z = ln_matmul(x2, g_mix, sh1, sc1, w_in.astype(BF16), t, act="gelu")
    return chunk_mlp_out(z, g_v, w_s.astype(BF16), b_s, w_out.astype(BF16), x2, gt1, t)


def _peer_layer(x2, t, mods, g_chan, w_q, keys, u_all, v_all, layer):
    sh2, sc2, gt2 = mods
    hn, idx, gate = peer_topk(x2, g_chan, sh2, sc2, w_q.T.astype(BF16),
                              keys.reshape(PEER_HEADS * 2, PEER_NKEYS, PEER_HALF), t)
    return peer_gather(idx, gate, hn, x2, gt2, pack_expert_table(u_all, v_all, layer), t)


def kernel(x, c, ctx, c_ctx, w_mod, b_mod, g_mix, g_chan, ml_w_in, ml_conv_w, ml_conv_b, ml_gate_b, ml_head_g, ml_w_out, cm_w_in, cm_g_v, cm_w_s, cm_b_s, cm_w_out, peer_w_q, peer_keys, peer_u, peer_v, g_final):
    bsz, t, d = x.shape
    depth = w_mod.shape[0]
    cond = jnp.zeros((8, d), F32).at[:bsz].set(c).at[bsz].set(c_ctx)
    mods = modulation(cond, w_mod, b_mod)
    h2 = x.reshape(bsz * t, d)
    for i in range(depth):
        j = i // 2
        mi = mods[i].reshape(8, 6, d)
        lat = [mi[:bsz, s] for s in range(6)]
        cx = [mi[bsz:bsz + 1, s] for s in range(6)]
        if i % 2 == 0:
            h2 = _mlstm_layer(h2.reshape(bsz, t, d), ctx, lat[:3], cx[:2], g_mix[i], ml_w_in[j], ml_conv_w[j],
                              ml_conv_b[j], ml_gate_b[j], ml_head_g[j], ml_w_out[j])
        else:
            h2 = _chunk_mlp_layer(h2, t, lat[:3], g_mix[i], cm_w_in[j], cm_g_v[j], cm_w_s[j], cm_b_s[j],
                                  cm_w_out[j])
        h2 = _peer_layer(h2, t, lat[3:], g_chan[i], peer_w_q[i], peer_keys[i], peer_u, peer_v, i)
    return rmsnorm(h2, g_final).reshape(bsz, t, d)
```
